```python
import math
import jax, jax.numpy as jnp
from jax import lax
import numpy as np

D_MODEL = 1024
BATCH = 16
SEQ = 2048
DEPTH = 2

NORM_EPS = 1e-6
ROPE_THETA = 500000.0
HEAD_DIM = 64
ROT_DIM = HEAD_DIM // 4
NEG_INF = -1e30
Q_BLOCK = 128
RNN_WIDTH = D_MODEL // 2
RNN_BLOCKS = 8
RNN_BLOCK = RNN_WIDTH // RNN_BLOCKS
CONV_WIDTH = 4
RGLRU_C = 8.0
DIFF_HEADS = (D_MODEL // 2) // (2 * HEAD_DIM)
DIFF_QK_WIDTH = DIFF_HEADS * 2 * HEAD_DIM
DIFF_V_DIM = 2 * HEAD_DIM
DIFF_WIDTH = DIFF_HEADS * DIFF_V_DIM
AB_IN_WIDTH = 2 * RNN_WIDTH + 2 * DIFF_QK_WIDTH + DIFF_WIDTH
DIL_HEADS = D_MODEL // HEAD_DIM
DIL_PATTERNS = ((128, 1), (512, 4), (2048, 16))
D_FF = -(-8 * D_MODEL // (3 * 256)) * 256
N_EVEN = (DEPTH + 1) // 2
N_ODD = DEPTH // 2

kernel_name = "hawk_diffattn_dilated_hybrid"


def rms_norm(x, g):
    xf = x.astype(jnp.float32)
    y = xf * lax.rsqrt(jnp.mean(xf * xf, axis=-1, keepdims=True) + NORM_EPS)
    return (y * g.astype(jnp.float32)).astype(x.dtype)


def rope_tables(seq_len):
    pos = jnp.arange(seq_len, dtype=jnp.float32)
    inv_freq = 1.0 / (ROPE_THETA ** (jnp.arange(0, ROT_DIM, 2, dtype=jnp.float32) / ROT_DIM))
    ang = pos[:, None] * inv_freq[None, :]
    return jnp.cos(ang), jnp.sin(ang)


def apply_partial_rope(t, cos, sin):
    shape = (cos.shape[0],) + (1,) * (t.ndim - 3) + (cos.shape[1],)
    c = cos.reshape(shape)
    s = sin.reshape(shape)
    tf = t.astype(jnp.float32)
    half = ROT_DIM // 2
    x1 = tf[..., :half]
    x2 = tf[..., half:ROT_DIM]
    out = jnp.concatenate([x1 * c - x2 * s, x2 * c + x1 * s, tf[..., ROT_DIM:]], axis=-1)
    return out.astype(t.dtype)


def rglru_block(xr, gate, conv_w, conv_b, wa, ba, wx, bx, lru_lambda):
    B, S, W = xr.shape
    xp = jnp.pad(xr, ((0, 0), (CONV_WIDTH - 1, 0), (0, 0)))
    u = conv_b + sum(xp[:, j:j + S, :] * conv_w[j] for j in range(CONV_WIDTH))
    ub = u.reshape(B, S, RNN_BLOCKS, RNN_BLOCK)
    r = jax.nn.sigmoid((jnp.einsum('bsgi,gij->bsgj', ub, wa).reshape(B, S, W) + ba).astype(jnp.float32))
    i = jax.nn.sigmoid((jnp.einsum('bsgi,gij->bsgj', ub, wx).reshape(B, S, W) + bx).astype(jnp.float32))
    log_a = -RGLRU_C * r * jax.nn.softplus(-lru_lambda.astype(jnp.float32))
    a = jnp.exp(log_a)
    b = jnp.sqrt(-jnp.expm1(2.0 * log_a)) * (i * u.astype(jnp.float32))

    def combine(left, right):
        a1, b1 = left
        a2, b2 = right
        return a1 * a2, a2 * b1 + b2

    _, h = lax.associative_scan(combine, (a, b), axis=1)
    return (h * jax.nn.gelu(gate.astype(jnp.float32))).astype(xr.dtype)


def diff_attention(q, k, v, lam, lambda_init, subln_g):
    B, S, H, _, Dh = q.shape
    nb = S // Q_BLOCK
    qb = q.reshape(B, nb, Q_BLOCK, H, 2, Dh).transpose(1, 0, 2, 3, 4, 5)
    kf = k.astype(jnp.float32)
    vf = v.astype(jnp.float32)
    k_pos = jnp.arange(S)

    def one_block(args):
        qblk, bi = args
        s = jnp.einsum('bqhcd,bkhcd->bhcqk', qblk.astype(jnp.float32), kf)
        q_pos = bi * Q_BLOCK + jnp.arange(Q_BLOCK)
        mask = k_pos[None, :] <= q_pos[:, None]
        p = jax.nn.softmax(jnp.where(mask, s, NEG_INF), axis=-1)
        w = p[:, :, 0] - lam * p[:, :, 1]
        return jnp.einsum('bhqk,bkhd->bqhd', w, vf)

    o = lax.map(one_block, (qb, jnp.arange(nb)))
    o = o.transpose(1, 0, 2, 3, 4).reshape(B, S, H, 2 * Dh)
    o = rms_norm(o, subln_g) * (1.0 - lambda_init)
    return o.reshape(B, S, H * 2 * Dh).astype(v.dtype)


def dilated_window_attn(q, k, v, window, dilation):
    B, S, H, Dh = q.shape
    blk = window // dilation
    unit = blk * dilation
    s_pad = -(-S // unit) * unit
    lc = s_pad // dilation
    nb = lc // blk

    def to_blocks(t):
        t = jnp.pad(t.astype(jnp.float32), ((0, 0), (0, s_pad - S), (0, 0), (0, 0)))
        t = t.reshape(B, lc, dilation, H, Dh).transpose(0, 2, 3, 1, 4)
        return t.reshape(B, dilation, H, nb, blk, Dh)

    def with_prev(t):
        prev = jnp.pad(t[:, :, :, :-1], ((0, 0), (0, 0), (0, 0), (1, 0), (0, 0), (0, 0)))
        return jnp.concatenate([prev, t], axis=4)

    qb = to_blocks(q)
    k2 = with_prev(to_blocks(k))
    v2 = with_prev(to_blocks(v))
    s = jnp.einsum('brhnqd,brhnkd->brhnqk', qb, k2)
    qi = jnp.arange(blk)[:, None] + blk
    ki = jnp.arange(2 * blk)[None, :]
    dist = qi - ki
    band = (dist >= 0) & (dist <= blk)
    has_prev = (jnp.arange(nb)[:, None, None] > 0) | (ki[None] >= blk)
    mask = band[None] & has_prev
    s = jnp.where(mask, s, NEG_INF)
    m = jnp.max(s, axis=-1, keepdims=True)
    e = jnp.exp(s - m)
    den = jnp.sum(e, axis=-1)
    o = jnp.einsum('brhnqk,brhnkd->brhnqd', e, v2) / den[..., None]
    lse = m[..., 0] + jnp.log(den)
    o = o.reshape(B, dilation, H, lc, Dh).transpose(0, 3, 1, 2, 4).reshape(B, s_pad, H, Dh)[:, :S]
    lse = lse.reshape(B, dilation, H, lc).transpose(0, 3, 1, 2).reshape(B, s_pad, H)[:, :S]
    return o, lse


def hawk_diff_mixer(x, cos, sin, layer_idx, norm_g, w_in, conv_w, conv_b, wa, ba, wx, bx,
                    lru_lambda, q_norm_g, k_norm_g, lq1, lk1, lq2, lk2, subln_g, w_out):
    B, S, _ = x.shape
    h = rms_norm(x, norm_g)
    proj = h @ w_in
    cuts = np.cumsum([RNN_WIDTH, RNN_WIDTH, DIFF_QK_WIDTH, DIFF_QK_WIDTH]).tolist()
    xr, gate, q, k, v = jnp.split(proj, cuts, axis=-1)
    y_rnn = rglru_block(xr, gate, conv_w, conv_b, wa, ba, wx, bx, lru_lambda)
    q = q.reshape(B, S, DIFF_HEADS, 2, HEAD_DIM)
    k = k.reshape(B, S, DIFF_HEADS, 2, HEAD_DIM)
    v = v.reshape(B, S, DIFF_HEADS, DIFF_V_DIM)
    q = apply_partial_rope(rms_norm(q, q_norm_g), cos, sin) * (HEAD_DIM ** -0.5)
    k = apply_partial_rope(rms_norm(k, k_norm_g), cos, sin)
    lambda_init = 0.8 - 0.6 * math.exp(-0.3 * layer_idx)
    f32 = jnp.float32
    lam = (jnp.exp(jnp.sum(lq1.astype(f32) * lk1.astype(f32)))
           - jnp.exp(jnp.sum(lq2.astype(f32) * lk2.astype(f32))) + lambda_init)
    y_diff = diff_attention(q, k, v, lam, lambda_init, subln_g)
    return jnp.concatenate([y_rnn, y_diff], axis=-1) @ w_out


def dilated_mixer(x, cos, sin, norm_g, w_qkv, q_norm_g, k_norm_g, w_out):
    B, S, _ = x.shape
    h = rms_norm(x, norm_g)
    q, k, v = jnp.split(h @ w_qkv, 3, axis=-1)
    q = q.reshape(B, S, DIL_HEADS, HEAD_DIM)
    k = k.reshape(B, S, DIL_HEADS, HEAD_DIM)
    v = v.reshape(B, S, DIL_HEADS, HEAD_DIM)
    q = apply_partial_rope(rms_norm(q, q_norm_g), cos, sin) * (HEAD_DIM ** -0.5)
    k = apply_partial_rope(rms_norm(k, k_norm_g), cos, sin)
    outs = []
    lses = []
    for window, dilation in DIL_PATTERNS:
        o, lse = dilated_window_attn(q, k, v, window, dilation)
        outs.append(o)
        lses.append(lse)
    alpha = jax.nn.softmax(jnp.stack(lses), axis=0)
    o = jnp.einsum('gbsh,gbshd->bshd', alpha, jnp.stack(outs))
    return o.reshape(B, S, D_MODEL).astype(x.dtype) @ w_out


def swiglu(x, norm_g, w_gate, w_up, w_down):
    h = rms_norm(x, norm_g)
    return (jax.nn.silu(h @ w_gate) * (h @ w_up)) @ w_down


def setup_inputs(seed: int = 0) -> dict:
    key = jax.random.key(seed)
    ks = jax.random.split(key, 32)
    f32 = jnp.float32

    def normal(k, shape, scale):
        return jax.random.normal(k, shape, f32) * scale

    def gain(k, shape):
        return 1.0 + 0.02 * jax.random.normal(k, shape, f32)

    u = jax.random.uniform(ks[9], (N_EVEN, RNN_WIDTH), f32, minval=0.9, maxval=0.999)
    a0 = u ** (1.0 / RGLRU_C)
    lru_lambda = jnp.log(a0) - jnp.log1p(-a0)
    return {
        "x": jax.random.normal(ks[0], (BATCH, SEQ, D_MODEL), f32),
        "ab_norm_g": gain(ks[1], (N_EVEN, D_MODEL)),
        "ab_w_in": normal(ks[2], (N_EVEN, D_MODEL, AB_IN_WIDTH), D_MODEL ** -0.5),
        "ab_conv_w": normal(ks[3], (N_EVEN, CONV_WIDTH, RNN_WIDTH), CONV_WIDTH ** -0.5),
        "ab_conv_b": normal(ks[4], (N_EVEN, RNN_WIDTH), 0.01),
        "ab_wa": normal(ks[5], (N_EVEN, RNN_BLOCKS, RNN_BLOCK, RNN_BLOCK), RNN_BLOCK ** -0.5),
        "ab_ba": normal(ks[6], (N_EVEN, RNN_WIDTH), 0.01),
        "ab_wx": normal(ks[7], (N_EVEN, RNN_BLOCKS, RNN_BLOCK, RNN_BLOCK), RNN_BLOCK ** -0.5),
        "ab_bx": normal(ks[8], (N_EVEN, RNN_WIDTH), 0.01),
        "ab_lru_lambda": lru_lambda,
        "ab_q_norm_g": gain(ks[10], (N_EVEN, HEAD_DIM)),
        "ab_k_norm_g": gain(ks[11], (N_EVEN, HEAD_DIM)),
        "ab_lambda_q1": normal(ks[12], (N_EVEN, HEAD_DIM), 0.1),
        "ab_lambda_k1": normal(ks[13], (N_EVEN, HEAD_DIM), 0.1),
        "ab_lambda_q2": normal(ks[14], (N_EVEN, HEAD_DIM), 0.1),
        "ab_lambda_k2": normal(ks[15], (N_EVEN, HEAD_DIM), 0.1),
        "ab_subln_g": gain(ks[16], (N_EVEN, DIFF_V_DIM)),
        "ab_w_out": normal(ks[17], (N_EVEN, RNN_WIDTH + DIFF_WIDTH, D_MODEL), (RNN_WIDTH + DIFF_WIDTH) ** -0.5),
        "c_norm_g": gain(ks[18], (N_ODD, D_MODEL)),
        "c_w_qkv": normal(ks[19], (N_ODD, D_MODEL, 3 * D_MODEL), D_MODEL ** -0.5),
        "c_q_norm_g": gain(ks[20], (N_ODD, HEAD_DIM)),
        "c_k_norm_g": gain(ks[21], (N_ODD, HEAD_DIM)),
        "c_w_out": normal(ks[22], (N_ODD, D_MODEL, D_MODEL), D_MODEL ** -0.5),
        "ffn_norm_g": gain(ks[23], (DEPTH, D_MODEL)),
        "ffn_w_gate": normal(ks[24], (DEPTH, D_MODEL, D_FF), D_MODEL ** -0.5),
        "ffn_w_up": normal(ks[25], (DEPTH, D_MODEL, D_FF), D_MODEL ** -0.5),
        "ffn_w_down": normal(ks[26], (DEPTH, D_FF, D_MODEL), D_FF ** -0.5),
    }


def reference(x, ab_norm_g, ab_w_in, ab_conv_w, ab_conv_b, ab_wa, ab_ba, ab_wx, ab_bx,
              ab_lru_lambda, ab_q_norm_g, ab_k_norm_g, ab_lambda_q1, ab_lambda_k1,
              ab_lambda_q2, ab_lambda_k2, ab_subln_g, ab_w_out, c_norm_g, c_w_qkv,
              c_q_norm_g, c_k_norm_g, c_w_out, ffn_norm_g, ffn_w_gate, ffn_w_up, ffn_w_down):
    cos, sin = rope_tables(x.shape[1])
    for layer in range(DEPTH):
        j = layer // 2
        if layer % 2 == 0:
            x = x + hawk_diff_mixer(x, cos, sin, layer, ab_norm_g[j], ab_w_in[j], ab_conv_w[j],
                                    ab_conv_b[j], ab_wa[j], ab_ba[j], ab_wx[j], ab_bx[j],
                                    ab_lru_lambda[j], ab_q_norm_g[j], ab_k_norm_g[j],
                                    ab_lambda_q1[j], ab_lambda_k1[j], ab_lambda_q2[j],
                                    ab_lambda_k2[j], ab_subln_g[j], ab_w_out[j])
        else:
            x = x + dilated_mixer(x, cos, sin, c_norm_g[j], c_w_qkv[j], c_q_norm_g[j],
                                  c_k_norm_g[j], c_w_out[j])
        x = x + swiglu(x, ffn_norm_g[layer], ffn_w_gate[layer], ffn_w_up[layer], ffn_w_down[layer])
    return x
```

```python
import functools
import math

import jax
import jax.numpy as jnp
from jax import lax
from jax.experimental import pallas as pl
from jax.experimental.pallas import tpu as pltpu

D_MODEL = 1024
NORM_EPS = 1e-6
ROPE_THETA = 500000.0
HEAD_DIM = 64
ROT_DIM = HEAD_DIM // 4
ROT_HALF = ROT_DIM // 2
NEG_INF = -1e30
RNN_WIDTH = D_MODEL // 2
RNN_BLOCKS = 8
RNN_BLOCK = RNN_WIDTH // RNN_BLOCKS
CONV_WIDTH = 4
RGLRU_C = 8.0
DIFF_HEADS = (D_MODEL // 2) // (2 * HEAD_DIM)
DIFF_QK_WIDTH = DIFF_HEADS * 2 * HEAD_DIM
DIFF_V_DIM = 2 * HEAD_DIM
DIFF_WIDTH = DIFF_HEADS * DIFF_V_DIM
DIL_HEADS = D_MODEL // HEAD_DIM
DIL_PATTERNS = ((128, 1), (512, 4), (2048, 16))
DIL_BLOCK = 128
D_FF = -(-8 * D_MODEL // (3 * 256)) * 256

LANES = 128
SUBLANES = 8
MXU_DIM = 256
VMEM_LIMIT_BYTES = 56 * 1024 * 1024

F32 = jnp.float32
BF16 = jnp.bfloat16


def _dot(a, b):
    return jnp.dot(a, b, preferred_element_type=F32)


def _dot_nt(a, b):
    return lax.dot_general(a, b, (((1,), (1,)), ((), ())), preferred_element_type=F32)


def _rms(xf, g):
    ms = jnp.mean(xf * xf, axis=-1, keepdims=True)
    return xf * lax.rsqrt(ms + NORM_EPS) * g


def _const_spec(shape):
    nd = len(shape)
    return pl.BlockSpec(shape, lambda *_: (0,) * nd, pipeline_mode=pl.Buffered(1))


def _head_pair_norm_rope(t, g2, cos_t, sin_a, sin_b):
    lo = lax.broadcasted_iota(jnp.int32, t.shape, 1) < HEAD_DIM
    sq = t * t
    s_lo = jnp.sum(jnp.where(lo, sq, 0.0), axis=-1, keepdims=True)
    s_hi = jnp.sum(jnp.where(lo, 0.0, sq), axis=-1, keepdims=True)
    inv = lax.rsqrt(jnp.where(lo, s_lo, s_hi) * (1.0 / HEAD_DIM) + NORM_EPS)
    y = t * inv * g2
    return (y * cos_t + pltpu.roll(y, ROT_HALF, 1) * sin_a
            + pltpu.roll(y, LANES - ROT_HALF, 1) * sin_b)


def _rope_tables(seq_len):
    pos = jnp.arange(seq_len, dtype=F32)
    inv_freq = 1.0 / (ROPE_THETA ** (jnp.arange(0, ROT_DIM, 2, dtype=F32) / ROT_DIM))
    ang = pos[:, None] * inv_freq[None, :]
    cos, sin = jnp.cos(ang), jnp.sin(ang)
    rest = HEAD_DIM - ROT_DIM
    one_r = jnp.ones((seq_len, rest), F32)
    zero_r = jnp.zeros((seq_len, rest), F32)
    zero_h = jnp.zeros((seq_len, ROT_HALF), F32)
    cos_t = jnp.concatenate([cos, cos, one_r], axis=-1)
    sin_a = jnp.concatenate([zero_h, sin, zero_r], axis=-1)
    sin_b = jnp.concatenate([-sin, zero_h, zero_r], axis=-1)
    pair = lambda t: jnp.concatenate([t, t], axis=-1)
    return pair(cos_t), pair(sin_a), pair(sin_b)


def _proj_kernel(x_ref, g_ref, w_ref, qg_ref, kg_ref, cos_ref, sa_ref, sb_ref, *out_refs,
                 plain_w, qk_w, v_w):
    h = _rms(x_ref[...], g_ref[...]).astype(BF16)
    if plain_w:
        plain_ref, q_ref, k_ref, v_ref = out_refs
        plain_ref[...] = _dot(h, w_ref[:, 0:plain_w])
    else:
        q_ref, k_ref, v_ref = out_refs
    cos_t, sin_a, sin_b = cos_ref[...], sa_ref[...], sb_ref[...]
    q_scale = HEAD_DIM ** -0.5
    for ref, gref, off, scale in ((q_ref, qg_ref, plain_w, q_scale),
                                  (k_ref, kg_ref, plain_w + qk_w, None)):
        t_all = _dot(h, w_ref[:, off:off + qk_w])
        g2 = gref[...]
        for c in range(qk_w // LANES):
            y = _head_pair_norm_rope(t_all[:, c * LANES:(c + 1) * LANES], g2, cos_t, sin_a, sin_b)
            if scale is not None:
                y = y * scale
            ref[:, c * LANES:(c + 1) * LANES] = y.astype(BF16)
    off = plain_w + 2 * qk_w
    v_ref[...] = _dot(h, w_ref[:, off:off + v_w]).astype(BF16)


def _project(x2, seq_len, norm_g, w, q_gain, k_gain, tables, *, plain_w, qk_w, v_w, tm):
    n, d = x2.shape
    n_out = plain_w + 2 * qk_w + v_w
    assert w.shape == (d, n_out) and n % tm == 0 and seq_len % tm == 0
    tiles_per_seq = seq_len // tm
    row = lambda i: (i, 0)
    pos = lambda i: (i % tiles_per_seq, 0)
    pair = lambda g: jnp.concatenate([g, g]).reshape(1, LANES).astype(F32)
    out_shape, out_specs = [], []
    if plain_w:
        out_shape.append(jax.ShapeDtypeStruct((n, plain_w), F32))
        out_specs.append(pl.BlockSpec((tm, plain_w), row))
    for width in (qk_w, qk_w, v_w):
        out_shape.append(jax.ShapeDtypeStruct((n, width), BF16))
        out_specs.append(pl.BlockSpec((tm, width), row))
    return pl.pallas_call(
        functools.partial(_proj_kernel, plain_w=plain_w, qk_w=qk_w, v_w=v_w),
        grid=(n // tm,),
        in_specs=[pl.BlockSpec((tm, d), row), _const_spec((1, d)), _const_spec((d, n_out)),
                  _const_spec((1, LANES)), _const_spec((1, LANES)),
                  pl.BlockSpec((tm, LANES), pos), pl.BlockSpec((tm, LANES), pos),
                  pl.BlockSpec((tm, LANES), pos)],
        out_specs=out_specs,
        out_shape=out_shape,
        compiler_params=pltpu.CompilerParams(dimension_semantics=("arbitrary",),
                                             vmem_limit_bytes=VMEM_LIMIT_BYTES),
        name="proj_qk_rope",
    )(x2, norm_g.reshape(1, d), w.astype(BF16), pair(q_gain), pair(k_gain), *tables)


def _rglru_kernel(xr_ref, gate_ref, cw_ref, cb_ref, wax_ref, ba_ref, bx_ref, lam_ref,
                  y_ref, tail_scr, h_scr, *, ts):
    @pl.when(pl.program_id(1) == 0)
    def _():
        tail_scr[...] = jnp.zeros_like(tail_scr)
        h_scr[...] = jnp.zeros_like(h_scr)

    x = xr_ref[0]
    xe = jnp.concatenate([tail_scr[...], x], axis=0)
    cw = cw_ref[...]
    u = cb_ref[...] + x * cw[CONV_WIDTH - 1:CONV_WIDTH]
    for back in range(1, CONV_WIDTH):
        shifted = pltpu.roll(xe, back, 0)[SUBLANES:]
        u = u + shifted * cw[CONV_WIDTH - 1 - back:CONV_WIDTH - back]
    tail_scr[...] = x[ts - SUBLANES:]

    ub = u.astype(BF16)
    half = RNN_WIDTH // 2
    ra0 = _dot(ub[:, :half], wax_ref[0])
    ra1 = _dot(ub[:, half:], wax_ref[1])
    r = jax.nn.sigmoid(jnp.concatenate([ra0[:, :half], ra1[:, :half]], axis=-1) + ba_ref[...])
    i = jax.nn.sigmoid(jnp.concatenate([ra0[:, half:], ra1[:, half:]], axis=-1) + bx_ref[...])
    z = -lam_ref[...]
    softplus = jnp.maximum(z, 0.0) + jnp.log1p(jnp.exp(-jnp.abs(z)))
    log_a = (-RGLRU_C) * r * softplus
    a = jnp.exp(log_a)
    b = jnp.sqrt(-jnp.tanh(log_a) * (a * a + 1.0)) * (i * u)

    rows = lax.broadcasted_iota(jnp.int32, a.shape, 0)
    step = 1
    while step < ts:
        keep = rows >= step
        a_prev = jnp.where(keep, pltpu.roll(a, step, 0), 1.0)
        b_prev = jnp.where(keep, pltpu.roll(b, step, 0), 0.0)
        b = a * b_prev + b
        a = a * a_prev
        step *= 2
    h = b + a * h_scr[0:1, :]
    h_scr[...] = jnp.broadcast_to(h[ts - 1:ts, :], h_scr.shape)
    y_ref[0] = (h * jax.nn.gelu(gate_ref[0])).astype(BF16)


def _rglru(xg3, conv_w, conv_b, wa, wx, ba, bx, lru_lambda, *, ts):
    bsz, seq_len, _ = xg3.shape
    w = RNN_WIDTH
    half = w // 2
    per_half = RNN_BLOCKS // 2

    def block_diag(wt):
        out = jnp.zeros((2, half, half), F32)
        for g in range(RNN_BLOCKS):
            j = (g % per_half) * RNN_BLOCK
            out = out.at[g // per_half, j:j + RNN_BLOCK, j:j + RNN_BLOCK].set(wt[g])
        return out

    wax = jnp.concatenate([block_diag(wa), block_diag(wx)], axis=-1).astype(BF16)
    vec = lambda v: v.reshape(1, w).astype(F32)
    return pl.pallas_call(
        functools.partial(_rglru_kernel, ts=ts),
        grid=(bsz, seq_len // ts),
        in_specs=[pl.BlockSpec((1, ts, w), lambda b, i: (b, i, 0)),
                  pl.BlockSpec((1, ts, w), lambda b, i: (b, i, 1)),
                  _const_spec((CONV_WIDTH, w)), _const_spec((1, w)),
                  _const_spec((2, half, 2 * half)),
                  _const_spec((1, w)), _const_spec((1, w)), _const_spec((1, w))],
        out_specs=pl.BlockSpec((1, ts, w), lambda b, i: (b, i, 0)),
        out_shape=jax.ShapeDtypeStruct((bsz, seq_len, w), BF16),
        scratch_shapes=[pltpu.VMEM((SUBLANES, w), F32), pltpu.VMEM((SUBLANES, w), F32)],
        compiler_params=pltpu.CompilerParams(dimension_semantics=("arbitrary", "arbitrary"),
                                             vmem_limit_bytes=VMEM_LIMIT_BYTES),
        name="rglru",
    )(xg3, xg3, conv_w.astype(F32), vec(conv_b), wax, vec(ba), vec(bx), vec(lru_lambda))


def _diff_attn_kernel(q_ref, k_ref, v_ref, lq1_ref, lk1_ref, lq2_ref, lk2_ref, sg_ref, o_ref,
                      *, tq, tk, lambda_init):
    qi = pl.program_id(2)
    q = q_ref[0]
    lo = lax.broadcasted_iota(jnp.int32, q.shape, 1) < HEAD_DIM
    zero = jnp.zeros_like(q)
    q_parts = (jnp.where(lo, q, zero), jnp.where(lo, zero, q))
    q_pos = qi * tq + lax.broadcasted_iota(jnp.int32, (tq, tk), 0)
    k_off = lax.broadcasted_iota(jnp.int32, (tq, tk), 1)

    def body(j, carry):
        start = pl.multiple_of(j * tk, tk)
        kb = k_ref[0, pl.ds(start, tk), :]
        vb = v_ref[0, pl.ds(start, tk), :]
        visible = (k_off + start) <= q_pos
        new = []
        for c in range(2):
            m, l, acc = carry[c]
            s = jnp.where(visible, _dot_nt(q_parts[c], kb), NEG_INF)
            m_new = jnp.maximum(m, jnp.max(s, axis=-1, keepdims=True))
            alpha = jnp.exp(m - m_new)
            e = jnp.exp(s - m_new)
            l = alpha * l + jnp.sum(e, axis=-1, keepdims=True)
            acc = alpha * acc + _dot(e.astype(BF16), vb)
            new.append((m_new, l, acc))
        return tuple(new)

    init = tuple((jnp.full((tq, 1), NEG_INF, F32), jnp.zeros((tq, 1), F32),
                  jnp.zeros((tq, DIFF_V_DIM), F32)) for _ in range(2))
    n_kv = (qi * tq + tq + tk - 1) // tk
    (_, l0, acc0), (_, l1, acc1) = lax.fori_loop(0, n_kv, body, init)

    dot_l = lambda a, b: jnp.sum(a[...] * b[...], axis=-1, keepdims=True)
    lam = jnp.exp(dot_l(lq1_ref, lk1_ref)) - jnp.exp(dot_l(lq2_ref, lk2_ref)) + lambda_init
    o = acc0 / l0 - lam * (acc1 / l1)
    o_ref[0] = (_rms(o, sg_ref[...]) * (1.0 - lambda_init)).astype(BF16)


def _diff_attention(q3, k3, v3, lq1, lk1, lq2, lk2, subln_g, *, layer_idx, tq, tk):
    bsz, seq_len, _ = q3.shape
    lambda_init = 0.8 - 0.6 * math.exp(-0.3 * layer_idx)
    vec = lambda v: v.reshape(1, -1).astype(F32)
    kv_spec = pl.BlockSpec((1, seq_len, LANES), lambda b, h, i: (b, 0, h))
    return pl.pallas_call(
        functools.partial(_diff_attn_kernel, tq=tq, tk=tk, lambda_init=lambda_init),
        grid=(bsz, DIFF_HEADS, seq_len // tq),
        in_specs=[pl.BlockSpec((1, tq, LANES), lambda b, h, i: (b, i, h)), kv_spec, kv_spec,
                  _const_spec((1, HEAD_DIM)), _const_spec((1, HEAD_DIM)),
                  _const_spec((1, HEAD_DIM)), _const_spec((1, HEAD_DIM)),
                  _const_spec((1, DIFF_V_DIM))],
        out_specs=pl.BlockSpec((1, tq, LANES), lambda b, h, i: (b, i, h)),
        out_shape=jax.ShapeDtypeStruct((bsz, seq_len, DIFF_WIDTH), BF16),
        compiler_params=pltpu.CompilerParams(
            dimension_semantics=("arbitrary", "arbitrary", "arbitrary"),
            vmem_limit_bytes=VMEM_LIMIT_BYTES),
        name="diff_attn",
    )(q3, k3, v3, vec(lq1), vec(lk1), vec(lq2), vec(lk2), vec(subln_g))


def _dil_unit(q_blk, k_cur, v_cur, k_prev, v_prev, prev_ok):
    blk = DIL_BLOCK
    lo = lax.broadcasted_iota(jnp.int32, (blk, LANES), 1) < HEAD_DIM
    qi = lax.broadcasted_iota(jnp.int32, (blk, blk), 0)
    ki = lax.broadcasted_iota(jnp.int32, (blk, blk), 1)
    cur_ok = ki <= qi
    zero = jnp.zeros_like(q_blk)
    stats = []
    for hd in range(2):
        qh = jnp.where(lo, q_blk, zero) if hd == 0 else jnp.where(lo, zero, q_blk)
        s_cur = jnp.where(cur_ok, _dot_nt(qh, k_cur), NEG_INF)
        m = jnp.max(s_cur, axis=-1, keepdims=True)
        if k_prev is not None:
            s_prev = jnp.where(jnp.logical_and(ki >= qi, prev_ok), _dot_nt(qh, k_prev), NEG_INF)
            m = jnp.maximum(m, jnp.max(s_prev, axis=-1, keepdims=True))
        e_cur = jnp.exp(s_cur - m)
        l = jnp.sum(e_cur, axis=-1, keepdims=True)
        acc = _dot(e_cur.astype(BF16), v_cur)
        if k_prev is not None:
            e_prev = jnp.exp(s_prev - m)
            l = l + jnp.sum(e_prev, axis=-1, keepdims=True)
            acc = acc + _dot(e_prev.astype(BF16), v_prev)
        stats.append((acc, m, l))
    (a0, m0, l0), (a1, m1, l1) = stats
    return (jnp.where(lo, a0, a1), jnp.where(lo, m0, m1), jnp.where(lo, l0, l1))


def _dil_attn_kernel(q_ref, k_ref, v_ref, o_ref, qf, kf, vf, qc, kc, vc,
                     acc_c, m_c, l_c, *nat, seq_len):
    blk = DIL_BLOCK
    n_pat = len(DIL_PATTERNS)
    acc_n, m_n, l_n = nat[:n_pat], nat[n_pat:2 * n_pat], nat[2 * n_pat:]
    qf[...] = q_ref[0].astype(F32)
    kf[...] = k_ref[0].astype(F32)
    vf[...] = v_ref[0].astype(F32)

    for g, (window, dil) in enumerate(DIL_PATTERNS):
        assert window // dil == blk and seq_len % (blk * dil) == 0
        lc = seq_len // dil
        nb = lc // blk
        for r in range(dil):
            if dil == 1:
                src_q, src_k, src_v = qf, kf, vf
            else:
                rows = pl.ds(r, lc, stride=dil)
                qc[0:lc, :] = qf[rows, :]
                kc[0:lc, :] = kf[rows, :]
                vc[0:lc, :] = vf[rows, :]
                src_q, src_k, src_v = qc, kc, vc

            def block(n, first):
                start = n * blk if first else pl.multiple_of(n * blk, blk)
                q_blk = src_q[pl.ds(start, blk), :].astype(BF16)
                k_cur = src_k[pl.ds(start, blk), :].astype(BF16)
                v_cur = src_v[pl.ds(start, blk), :].astype(BF16)
                if first:
                    acc, m, l = _dil_unit(q_blk, k_cur, v_cur, None, None, None)
                else:
                    prev = pl.multiple_of(start - blk, blk)
                    k_prev = src_k[pl.ds(prev, blk), :].astype(BF16)
                    v_prev = src_v[pl.ds(prev, blk), :].astype(BF16)
                    acc, m, l = _dil_unit(q_blk, k_cur, v_cur, k_prev, v_prev, True)
                acc_c[pl.ds(start, blk), :] = acc
                m_c[pl.ds(start, blk), :] = m
                l_c[pl.ds(start, blk), :] = l

            block(0, True)
            if nb > 1:
                def loop_body(n, carry):
                    block(n, False)
                    return carry
                lax.fori_loop(1, nb, loop_body, 0)

            if dil == 1:
                dst = pl.ds(0, lc)
            else:
                dst = pl.ds(r, lc, stride=dil)
            acc_n[g][dst, :] = acc_c[0:lc, :]
            m_n[g][dst, :] = m_c[0:lc, :]
            l_n[g][dst, :] = l_c[0:lc, :]

    chunk = 256
    for c in range(seq_len // chunk):
        sl = slice(c * chunk, (c + 1) * chunk)
        ms = [m_n[g][sl, :] for g in range(n_pat)]
        m_all = functools.reduce(jnp.maximum, ms)
        num = jnp.zeros((chunk, LANES), F32)
        den = jnp.zeros((chunk, LANES), F32)
        for g in range(n_pat):
            wgt = jnp.exp(ms[g] - m_all)
            num = num + wgt * acc_n[g][sl, :]
            den = den + wgt * l_n[g][sl, :]
        o_ref[0, sl, :] = (num / den).astype(BF16)


def _dilated_attention(q3, k3, v3):
    bsz, seq_len, width = q3.shape
    spec = pl.BlockSpec((1, seq_len, LANES), lambda b, h: (b, 0, h))
    n_pat = len(DIL_PATTERNS)
    seq_buf = lambda: pltpu.VMEM((seq_len, LANES), F32)
    return pl.pallas_call(
        functools.partial(_dil_attn_kernel, seq_len=seq_len),
        grid=(bsz, width // LANES),
        in_specs=[spec, spec, spec],
        out_specs=spec,
        out_shape=jax.ShapeDtypeStruct((bsz, seq_len, width), BF16),
        scratch_shapes=[seq_buf() for _ in range(9 + 3 * n_pat)],
        compiler_params=pltpu.CompilerParams(dimension_semantics=("arbitrary", "arbitrary"),
                                             vmem_limit_bytes=VMEM_LIMIT_BYTES),
        name="dilated_attn",
    )(q3, k3, v3)


def _mix_ffn_kernel(*refs, n_y):
    x_ref = refs[0]
    y_refs = refs[1:1 + n_y]
    wo_ref, g_ref, wg_ref, wu_ref, wd_ref, o_ref = refs[1 + n_y:]
    x1 = x_ref[...]
    off = 0
    for y_ref in y_refs:
        width = y_ref.shape[1]
        x1 = x1 + _dot(y_ref[...], wo_ref[off:off + width, :])
        off += width
    h = _rms(x1, g_ref[...]).astype(BF16)
    gate = _dot(h, wg_ref[...])
    up = _dot(h, wu_ref[...])
    act = (gate * jax.nn.sigmoid(gate) * up).astype(BF16)
    o_ref[...] = x1 + _dot(act, wd_ref[...])


def _mix_ffn(x2, ys, w_out, norm_g, w_gate, w_up, w_down, *, tm):
    n, d = x2.shape
    f = w_gate.shape[1]
    row = lambda i: (i, 0)
    return pl.pallas_call(
        functools.partial(_mix_ffn_kernel, n_y=len(ys)),
        grid=(n // tm,),
        in_specs=[pl.BlockSpec((tm, d), row)]
                 + [pl.BlockSpec((tm, y.shape[1]), row) for y in ys]
                 + [_const_spec(w_out.shape), _const_spec((1, d)), _const_spec((d, f)),
                    _const_spec((d, f)), _const_spec((f, d))],
        out_specs=pl.BlockSpec((tm, d), row),
        out_shape=jax.ShapeDtypeStruct((n, d), F32),
        compiler_params=pltpu.CompilerParams(dimension_semantics=("arbitrary",),
                                             vmem_limit_bytes=VMEM_LIMIT_BYTES),
        name="outproj_swiglu",
    )(x2, *ys, w_out.astype(BF16), norm_g.reshape(1, d), w_gate.astype(BF16),
      w_up.astype(BF16), w_down.astype(BF16))


def kernel(x, ab_norm_g, ab_w_in, ab_conv_w, ab_conv_b, ab_wa, ab_ba, ab_wx, ab_bx, ab_lru_lambda, ab_q_norm_g, ab_k_norm_g, ab_lambda_q1, ab_lambda_k1, ab_lambda_q2, ab_lambda_k2, ab_subln_g, ab_w_out, c_norm_g, c_w_qkv, c_q_norm_g, c_k_norm_g, c_w_out, ffn_norm_g, ffn_w_gate, ffn_w_up, ffn_w_down):
    bsz, seq_len, d = x.shape
    n = bsz * seq_len
    tables = _rope_tables(seq_len)
    x2 = x.reshape(n, d)

    xg, q, k, v = _project(x2, seq_len, ab_norm_g[0], ab_w_in[0], ab_q_norm_g[0], ab_k_norm_g[0],
                           tables, plain_w=2 * RNN_WIDTH, qk_w=DIFF_QK_WIDTH, v_w=DIFF_WIDTH,
                           tm=512)
    y_rnn = _rglru(xg.reshape(bsz, seq_len, -1), ab_conv_w[0], ab_conv_b[0], ab_wa[0], ab_wx[0],
                   ab_ba[0], ab_bx[0], ab_lru_lambda[0], ts=256)
    to3 = lambda t: t.reshape(bsz, seq_len, -1)
    y_diff = _diff_attention(to3(q), to3(k), to3(v), ab_lambda_q1[0], ab_lambda_k1[0],
                             ab_lambda_q2[0], ab_lambda_k2[0], ab_subln_g[0],
                             layer_idx=0, tq=256, tk=256)
    x2 = _mix_ffn(x2, [y_rnn.reshape(n, -1), y_diff.reshape(n, -1)], ab_w_out[0], ffn_norm_g[0],
                  ffn_w_gate[0], ffn_w_up[0], ffn_w_down[0], tm=256)

    q, k, v = _project(x2, seq_len, c_norm_g[0], c_w_qkv[0], c_q_norm_g[0], c_k_norm_g[0],
                       tables, plain_w=0, qk_w=D_MODEL, v_w=D_MODEL, tm=512)
    o = _dilated_attention(to3(q), to3(k), to3(v))
    x2 = _mix_ffn(x2, [o.reshape(n, -1)], c_w_out[0], ffn_norm_g[1], ffn_w_gate[1], ffn_w_up[1],
                  ffn_w_down[1], tm=256)
    return x2.reshape(bsz, seq_len, d)
```

```python
import functools
import math

import jax
import jax.numpy as jnp
from jax import lax
from jax.experimental import pallas as pl
from jax.experimental.pallas import tpu as pltpu

D_MODEL = 1024
NORM_EPS = 1e-6
ROPE_THETA = 500000.0
HEAD_DIM = 64
ROT_DIM = HEAD_DIM // 4
ROT_HALF = ROT_DIM // 2
NEG_INF = -1e30
RNN_WIDTH = D_MODEL // 2
RNN_BLOCKS = 8
RNN_BLOCK = RNN_WIDTH // RNN_BLOCKS
CONV_WIDTH = 4
RGLRU_C = 8.0
DIFF_HEADS = (D_MODEL // 2) // (2 * HEAD_DIM)
DIFF_QK_WIDTH = DIFF_HEADS * 2 * HEAD_DIM
DIFF_V_DIM = 2 * HEAD_DIM
DIFF_WIDTH = DIFF_HEADS * DIFF_V_DIM
DIL_HEADS = D_MODEL // HEAD_DIM
DIL_PATTERNS = ((128, 1), (512, 4), (2048, 16))
DIL_BLOCK = 128
D_FF = -(-8 * D_MODEL // (3 * 256)) * 256

LANES = 128
SUBLANES = 8
MXU_DIM = 256
VMEM_LIMIT_BYTES = 56 * 1024 * 1024

F32 = jnp.float32
BF16 = jnp.bfloat16


def _dot(a, b):
    return jnp.dot(a, b, preferred_element_type=F32)


def _dot_nt(a, b):
    return lax.dot_general(a, b, (((1,), (1,)), ((), ())), preferred_element_type=F32)


def _rms(xf, g):
    ms = jnp.mean(xf * xf, axis=-1, keepdims=True)
    return xf * lax.rsqrt(ms + NORM_EPS) * g


def _const_spec(shape):
    nd = len(shape)
    return pl.BlockSpec(shape, lambda *_: (0,) * nd, pipeline_mode=pl.Buffered(1))


def _head_pair_norm_rope(t, g2, cos_t, sin_a, sin_b):
    lo = lax.broadcasted_iota(jnp.int32, t.shape, 1) < HEAD_DIM
    sq = t * t
    s_lo = jnp.sum(jnp.where(lo, sq, 0.0), axis=-1, keepdims=True)
    s_hi = jnp.sum(jnp.where(lo, 0.0, sq), axis=-1, keepdims=True)
    inv = lax.rsqrt(jnp.where(lo, s_lo, s_hi) * (1.0 / HEAD_DIM) + NORM_EPS)
    y = t * inv * g2
    return (y * cos_t + pltpu.roll(y, ROT_HALF, 1) * sin_a
            + pltpu.roll(y, LANES - ROT_HALF, 1) * sin_b)


def _rope_tables(seq_len):
    pos = jnp.arange(seq_len, dtype=F32)
    inv_freq = 1.0 / (ROPE_THETA ** (jnp.arange(0, ROT_DIM, 2, dtype=F32) / ROT_DIM))
    ang = pos[:, None] * inv_freq[None, :]
    cos, sin = jnp.cos(ang), jnp.sin(ang)
    rest = HEAD_DIM - ROT_DIM
    one_r = jnp.ones((seq_len, rest), F32)
    zero_r = jnp.zeros((seq_len, rest), F32)
    zero_h = jnp.zeros((seq_len, ROT_HALF), F32)
    cos_t = jnp.concatenate([cos, cos, one_r], axis=-1)
    sin_a = jnp.concatenate([zero_h, sin, zero_r], axis=-1)
    sin_b = jnp.concatenate([-sin, zero_h, zero_r], axis=-1)
    pair = lambda t: jnp.concatenate([t, t], axis=-1)
    return pair(cos_t), pair(sin_a), pair(sin_b)


def _proj_kernel(x_ref, g_ref, w_ref, qg_ref, kg_ref, cos_ref, sa_ref, sb_ref, *out_refs,
                 plain_w, qk_w, v_w):
    h = _rms(x_ref[...], g_ref[...]).astype(BF16)
    if plain_w:
        plain_ref, q_ref, k_ref, v_ref = out_refs
        plain_ref[...] = _dot(h, w_ref[:, 0:plain_w])
    else:
        q_ref, k_ref, v_ref = out_refs
    cos_t, sin_a, sin_b = cos_ref[...], sa_ref[...], sb_ref[...]
    q_scale = HEAD_DIM ** -0.5
    for ref, gref, off, scale in ((q_ref, qg_ref, plain_w, q_scale),
                                  (k_ref, kg_ref, plain_w + qk_w, None)):
        t_all = _dot(h, w_ref[:, off:off + qk_w])
        g2 = gref[...]
        for c in range(qk_w // LANES):
            y = _head_pair_norm_rope(t_all[:, c * LANES:(c + 1) * LANES], g2, cos_t, sin_a, sin_b)
            if scale is not None:
                y = y * scale
            ref[:, c * LANES:(c + 1) * LANES] = y.astype(BF16)
    off = plain_w + 2 * qk_w
    v_ref[...] = _dot(h, w_ref[:, off:off + v_w]).astype(BF16)


def _project(x2, seq_len, norm_g, w, q_gain, k_gain, tables, *, plain_w, qk_w, v_w, tm):
    n, d = x2.shape
    n_out = plain_w + 2 * qk_w + v_w
    assert w.shape == (d, n_out) and n % tm == 0 and seq_len % tm == 0
    tiles_per_seq = seq_len // tm
    row = lambda i: (i, 0)
    pos = lambda i: (i % tiles_per_seq, 0)
    pair = lambda g: jnp.concatenate([g, g]).reshape(1, LANES).astype(F32)
    out_shape, out_specs = [], []
    if plain_w:
        out_shape.append(jax.ShapeDtypeStruct((n, plain_w), F32))
        out_specs.append(pl.BlockSpec((tm, plain_w), row))
    for width in (qk_w, qk_w, v_w):
        out_shape.append(jax.ShapeDtypeStruct((n, width), BF16))
        out_specs.append(pl.BlockSpec((tm, width), row))
    return pl.pallas_call(
        functools.partial(_proj_kernel, plain_w=plain_w, qk_w=qk_w, v_w=v_w),
        grid=(n // tm,),
        in_specs=[pl.BlockSpec((tm, d), row), _const_spec((1, d)), _const_spec((d, n_out)),
                  _const_spec((1, LANES)), _const_spec((1, LANES)),
                  pl.BlockSpec((tm, LANES), pos), pl.BlockSpec((tm, LANES), pos),
                  pl.BlockSpec((tm, LANES), pos)],
        out_specs=out_specs,
        out_shape=out_shape,
        compiler_params=pltpu.CompilerParams(dimension_semantics=("arbitrary",),
                                             vmem_limit_bytes=VMEM_LIMIT_BYTES),
        name="proj_qk_rope",
    )(x2, norm_g.reshape(1, d), w.astype(BF16), pair(q_gain), pair(k_gain), *tables)


def _rglru_kernel(xr_ref, gate_ref, cw_ref, cb_ref, wax_ref, ba_ref, bx_ref, lam_ref,
                  y_ref, tail_scr, h_scr, *, ts):
    @pl.when(pl.program_id(1) == 0)
    def _():
        tail_scr[...] = jnp.zeros_like(tail_scr)
        h_scr[...] = jnp.zeros_like(h_scr)

    x = xr_ref[0]
    xe = jnp.concatenate([tail_scr[...], x], axis=0)
    cw = cw_ref[...]
    u = cb_ref[...] + x * cw[CONV_WIDTH - 1:CONV_WIDTH]
    for back in range(1, CONV_WIDTH):
        shifted = pltpu.roll(xe, back, 0)[SUBLANES:]
        u = u + shifted * cw[CONV_WIDTH - 1 - back:CONV_WIDTH - back]
    tail_scr[...] = x[ts - SUBLANES:]

    ub = u.astype(BF16)
    half = RNN_WIDTH // 2
    ra0 = _dot(ub[:, :half], wax_ref[0])
    ra1 = _dot(ub[:, half:], wax_ref[1])
    r = jax.nn.sigmoid(jnp.concatenate([ra0[:, :half], ra1[:, :half]], axis=-1) + ba_ref[...])
    i = jax.nn.sigmoid(jnp.concatenate([ra0[:, half:], ra1[:, half:]], axis=-1) + bx_ref[...])
    z = -lam_ref[...]
    softplus = jnp.maximum(z, 0.0) + jnp.log1p(jnp.exp(-jnp.abs(z)))
    log_a = (-RGLRU_C) * r * softplus
    a = jnp.exp(log_a)
    b = jnp.sqrt(-jnp.tanh(log_a) * (a * a + 1.0)) * (i * u)

    rows = lax.broadcasted_iota(jnp.int32, a.shape, 0)
    step = 1
    while step < ts:
        keep = rows >= step
        a_prev = jnp.where(keep, pltpu.roll(a, step, 0), 1.0)
        b_prev = jnp.where(keep, pltpu.roll(b, step, 0), 0.0)
        b = a * b_prev + b
        a = a * a_prev
        step *= 2
    h = b + a * h_scr[0:1, :]
    h_scr[...] = jnp.broadcast_to(h[ts - 1:ts, :], h_scr.shape)
    y_ref[0] = (h * jax.nn.gelu(gate_ref[0])).astype(BF16)


def _rglru(xg3, conv_w, conv_b, wa, wx, ba, bx, lru_lambda, *, ts):
    bsz, seq_len, _ = xg3.shape
    w = RNN_WIDTH
    half = w // 2
    per_half = RNN_BLOCKS // 2

    def block_diag(wt):
        out = jnp.zeros((2, half, half), F32)
        for g in range(RNN_BLOCKS):
            j = (g % per_half) * RNN_BLOCK
            out = out.at[g // per_half, j:j + RNN_BLOCK, j:j + RNN_BLOCK].set(wt[g])
        return out

    wax = jnp.concatenate([block_diag(wa), block_diag(wx)], axis=-1).astype(BF16)
    vec = lambda v: v.reshape(1, w).astype(F32)
    return pl.pallas_call(
        functools.partial(_rglru_kernel, ts=ts),
        grid=(bsz, seq_len // ts),
        in_specs=[pl.BlockSpec((1, ts, w), lambda b, i: (b, i, 0)),
                  pl.BlockSpec((1, ts, w), lambda b, i: (b, i, 1)),
                  _const_spec((CONV_WIDTH, w)), _const_spec((1, w)),
                  _const_spec((2, half, 2 * half)),
                  _const_spec((1, w)), _const_spec((1, w)), _const_spec((1, w))],
        out_specs=pl.BlockSpec((1, ts, w), lambda b, i: (b, i, 0)),
        out_shape=jax.ShapeDtypeStruct((bsz, seq_len, w), BF16),
        scratch_shapes=[pltpu.VMEM((SUBLANES, w), F32), pltpu.VMEM((SUBLANES, w), F32)],
        compiler_params=pltpu.CompilerParams(dimension_semantics=("arbitrary", "arbitrary"),
                                             vmem_limit_bytes=VMEM_LIMIT_BYTES),
        name="rglru",
    )(xg3, xg3, conv_w.astype(F32), vec(conv_b), wax, vec(ba), vec(bx), vec(lru_lambda))


def _diff_attn_kernel(q_ref, k_ref, v_ref, lq1_ref, lk1_ref, lq2_ref, lk2_ref, sg_ref, o_ref,
                      *, tq, tk, lambda_init):
    qi = pl.program_id(2)
    q = q_ref[0]
    lo = lax.broadcasted_iota(jnp.int32, q.shape, 1) < HEAD_DIM
    zero = jnp.zeros_like(q)
    q_parts = (jnp.where(lo, q, zero), jnp.where(lo, zero, q))
    q_pos = qi * tq + lax.broadcasted_iota(jnp.int32, (tq, tk), 0)
    k_off = lax.broadcasted_iota(jnp.int32, (tq, tk), 1)

    def body(j, carry):
        start = pl.multiple_of(j * tk, tk)
        kb = k_ref[0, pl.ds(start, tk), :]
        vb = v_ref[0, pl.ds(start, tk), :]
        visible = (k_off + start) <= q_pos
        new = []
        for c in range(2):
            m, l, acc = carry[c]
            s = jnp.where(visible, _dot_nt(q_parts[c], kb), NEG_INF)
            m_new = jnp.maximum(m, jnp.max(s, axis=-1, keepdims=True))
            alpha = jnp.exp(m - m_new)
            e = jnp.exp(s - m_new)
            l = alpha * l + jnp.sum(e, axis=-1, keepdims=True)
            acc = alpha * acc + _dot(e.astype(BF16), vb)
            new.append((m_new, l, acc))
        return tuple(new)

    init = tuple((jnp.full((tq, 1), NEG_INF, F32), jnp.zeros((tq, 1), F32),
                  jnp.zeros((tq, DIFF_V_DIM), F32)) for _ in range(2))
    n_kv = (qi * tq + tq + tk - 1) // tk
    (_, l0, acc0), (_, l1, acc1) = lax.fori_loop(0, n_kv, body, init)

    dot_l = lambda a, b: jnp.sum(a[...] * b[...], axis=-1, keepdims=True)
    lam = jnp.exp(dot_l(lq1_ref, lk1_ref)) - jnp.exp(dot_l(lq2_ref, lk2_ref)) + lambda_init
    o = acc0 / l0 - lam * (acc1 / l1)
    o_ref[0] = (_rms(o, sg_ref[...]) * (1.0 - lambda_init)).astype(BF16)


def _diff_attention(q3, k3, v3, lq1, lk1, lq2, lk2, subln_g, *, layer_idx, tq, tk):
    bsz, seq_len, _ = q3.shape
    lambda_init = 0.8 - 0.6 * math.exp(-0.3 * layer_idx)
    vec = lambda v: v.reshape(1, -1).astype(F32)
    kv_spec = pl.BlockSpec((1, seq_len, LANES), lambda b, h, i: (b, 0, h))
    return pl.pallas_call(
        functools.partial(_diff_attn_kernel, tq=tq, tk=tk, lambda_init=lambda_init),
        grid=(bsz, DIFF_HEADS, seq_len // tq),
        in_specs=[pl.BlockSpec((1, tq, LANES), lambda b, h, i: (b, i, h)), kv_spec, kv_spec,
                  _const_spec((1, HEAD_DIM)), _const_spec((1, HEAD_DIM)),
                  _const_spec((1, HEAD_DIM)), _const_spec((1, HEAD_DIM)),
                  _const_spec((1, DIFF_V_DIM))],
        out_specs=pl.BlockSpec((1, tq, LANES), lambda b, h, i: (b, i, h)),
        out_shape=jax.ShapeDtypeStruct((bsz, seq_len, DIFF_WIDTH), BF16),
        compiler_params=pltpu.CompilerParams(
            dimension_semantics=("arbitrary", "arbitrary", "arbitrary"),
            vmem_limit_bytes=VMEM_LIMIT_BYTES),
        name="diff_attn",
    )(q3, k3, v3, vec(lq1), vec(lk1), vec(lq2), vec(lk2), vec(subln_g))


def _dil_unit(q0, q1, k2, va, vb, band):
    lo = lax.broadcasted_iota(jnp.int32, (DIL_BLOCK, LANES), 1) < HEAD_DIM
    res, maxes = [], []
    for qh, vh in ((q0, va), (q1, vb)):
        s = jnp.where(band, _dot_nt(qh, k2), NEG_INF)
        m = s[:, :LANES]
        for c in range(1, s.shape[1] // LANES):
            m = jnp.maximum(m, s[:, c * LANES:(c + 1) * LANES])
        m = jnp.max(m, axis=-1, keepdims=True)
        e = jnp.exp(s - m).astype(BF16)
        res.append(_dot(e, vh))
        maxes.append(m)
    acc = jnp.where(lo, res[0], res[1])
    l = pltpu.roll(jnp.where(lo, res[1], res[0]), HEAD_DIM, 1)
    return acc, jnp.where(lo, maxes[0], maxes[1]), l


def _dil_attn_kernel(q_ref, k_ref, v_ref, o_ref, qf, kf, vf, q0c, q1c, kc, vac, vbc,
                     *nat, seq_len):
    blk = DIL_BLOCK
    n_pat = len(DIL_PATTERNS)
    acc_n, m_n, l_n = nat[:n_pat], nat[n_pat:2 * n_pat], nat[2 * n_pat:]
    qf[...] = q_ref[0].astype(F32)
    kf[...] = k_ref[0].astype(F32)
    vf[...] = v_ref[0].astype(F32)

    qi = lax.broadcasted_iota(jnp.int32, (blk, blk), 0)
    ki = lax.broadcasted_iota(jnp.int32, (blk, blk), 1)
    band_cur = ki <= qi
    band_two = jnp.concatenate([ki >= qi, band_cur], axis=1)

    for g, (window, dil) in enumerate(DIL_PATTERNS):
        assert window // dil == blk and seq_len % (blk * dil) == 0
        lc = seq_len // dil
        for r in range(dil):
            rows = pl.ds(r, lc, stride=dil) if dil > 1 else pl.ds(0, lc)
            qv, vv = qf[rows, :], vf[rows, :]
            lo = lax.broadcasted_iota(jnp.int32, qv.shape, 1) < HEAD_DIM
            q0c[0:lc, :] = jnp.where(lo, qv, 0.0).astype(BF16)
            q1c[0:lc, :] = jnp.where(lo, 0.0, qv).astype(BF16)
            kc[0:lc, :] = kf[rows, :].astype(BF16)
            vac[0:lc, :] = jnp.where(lo, vv, 1.0).astype(BF16)
            vbc[0:lc, :] = jnp.where(lo, 1.0, vv).astype(BF16)
            for n in range(lc // blk):
                cur = slice(n * blk, (n + 1) * blk)
                keys = slice(max(n - 1, 0) * blk, (n + 1) * blk)
                acc, m, l = _dil_unit(q0c[cur, :], q1c[cur, :], kc[keys, :], vac[keys, :],
                                      vbc[keys, :], band_two if n else band_cur)
                first = n * blk * dil + r
                dst = pl.ds(first, blk, stride=dil) if dil > 1 else pl.ds(first, blk)
                acc_n[g][dst, :] = acc
                m_n[g][dst, :] = m
                l_n[g][dst, :] = l

    chunk = 256
    for c in range(seq_len // chunk):
        sl = slice(c * chunk, (c + 1) * chunk)
        ms = [m_n[g][sl, :] for g in range(n_pat)]
        m_all = functools.reduce(jnp.maximum, ms)
        num = jnp.zeros((chunk, LANES), F32)
        den = jnp.zeros((chunk, LANES), F32)
        for g in range(n_pat):
            wgt = jnp.exp(ms[g] - m_all)
            num = num + wgt * acc_n[g][sl, :]
            den = den + wgt * l_n[g][sl, :]
        o_ref[0, sl, :] = (num / den).astype(BF16)


def _dil_fast_kernel(q_ref, k_ref, v_ref, o_ref, qf, kf, vf, *bufs, seq_len):
    blk = DIL_BLOCK
    n_pat = len(DIL_PATTERNS)
    cls = [bufs[5 * g:5 * g + 5] for g in range(n_pat)]
    nat = bufs[5 * n_pat:]
    acc_n, l_n = nat[:n_pat], nat[n_pat:]

    qi = lax.broadcasted_iota(jnp.int32, (2 * blk, blk), 0) % blk
    ki = lax.broadcasted_iota(jnp.int32, (2 * blk, blk), 1)
    band_cur = ki <= qi
    band_two = jnp.concatenate([ki >= qi, band_cur], axis=1)
    lo_blk = lax.broadcasted_iota(jnp.int32, (blk, LANES), 1) < HEAD_DIM

    if any(dil > 1 for _, dil in DIL_PATTERNS):
        qf[...] = q_ref[0].astype(F32)
        kf[...] = k_ref[0].astype(F32)
        vf[...] = v_ref[0].astype(F32)

    for g, (window, dil) in enumerate(DIL_PATTERNS):
        assert window // dil == blk and seq_len % (blk * dil) == 0
        lc = seq_len // dil
        q0c, q1c, kc, vac, vbc = cls[g]
        for r in range(dil):
            base = r * lc
            if dil > 1:
                rows = pl.ds(r, lc, stride=dil)
                qv, kv, vv = qf[rows, :], kf[rows, :], vf[rows, :]
            else:
                qv, kv, vv = q_ref[0], k_ref[0], v_ref[0]
            lo = lax.broadcasted_iota(jnp.int32, qv.shape, 1) < HEAD_DIM
            zero, one = jnp.zeros_like(qv), jnp.ones_like(vv)
            q0c[base:base + lc, :] = jnp.where(lo, qv, zero).astype(BF16)
            q1c[base:base + lc, :] = jnp.where(lo, zero, qv).astype(BF16)
            kc[base:base + lc, :] = kv.astype(BF16)
            vac[base:base + lc, :] = jnp.where(lo, vv, one).astype(BF16)
            vbc[base:base + lc, :] = jnp.where(lo, one, vv).astype(BF16)
            for n in range(lc // blk):
                cur = slice(base + n * blk, base + (n + 1) * blk)
                keys = slice(base + max(n - 1, 0) * blk, base + (n + 1) * blk)
                q2 = jnp.concatenate([q0c[cur, :], q1c[cur, :]], axis=0)
                s = _dot_nt(q2, kc[keys, :])
                e = jnp.exp(jnp.where(band_two if n else band_cur, s, NEG_INF)).astype(BF16)
                ra = _dot(e[:blk], vac[keys, :])
                rb = _dot(e[blk:], vbc[keys, :])
                first = n * blk * dil + r
                dst = pl.ds(first, blk, stride=dil) if dil > 1 else pl.ds(first, blk)
                acc_n[g][dst, :] = jnp.where(lo_blk, ra, rb)
                l_n[g][dst, :] = jnp.where(lo_blk, rb, ra)

    chunk = 256
    for c in range(seq_len // chunk):
        sl = slice(c * chunk, (c + 1) * chunk)
        num = functools.reduce(jnp.add, [acc_n[g][sl, :] for g in range(n_pat)])
        den = functools.reduce(jnp.add, [l_n[g][sl, :] for g in range(n_pat)])
        o_ref[0, sl, :] = (num / pltpu.roll(den, HEAD_DIM, 1)).astype(BF16)


DIL_FAST_SCORE_BOUND = 60.0


def _dilated_attention(q3, k3, v3, score_bound):
    bsz, seq_len, width = q3.shape
    spec = pl.BlockSpec((1, seq_len, LANES), lambda b, h: (b, 0, h))
    n_pat = len(DIL_PATTERNS)
    common = dict(
        grid=(bsz, width // LANES),
        in_specs=[spec, spec, spec],
        out_specs=spec,
        out_shape=jax.ShapeDtypeStruct((bsz, seq_len, width), BF16),
        compiler_params=pltpu.CompilerParams(dimension_semantics=("arbitrary", "arbitrary"),
                                             vmem_limit_bytes=VMEM_LIMIT_BYTES))
    seq_f32 = lambda count: [pltpu.VMEM((seq_len, LANES), F32) for _ in range(count)]
    seq_bf16 = lambda count: [pltpu.VMEM((seq_len, LANES), BF16) for _ in range(count)]
    fast = pl.pallas_call(
        functools.partial(_dil_fast_kernel, seq_len=seq_len),
        scratch_shapes=seq_f32(3) + seq_bf16(5 * n_pat) + seq_f32(2 * n_pat),
        name="dilated_attn_fast", **common)
    safe = pl.pallas_call(
        functools.partial(_dil_attn_kernel, seq_len=seq_len),
        scratch_shapes=seq_f32(3) + seq_bf16(5) + seq_f32(3 * n_pat),
        name="dilated_attn", **common)
    return lax.cond(score_bound <= DIL_FAST_SCORE_BOUND, fast, safe, q3, k3, v3)


def _mix_ffn_kernel(*refs, n_y):
    x_ref = refs[0]
    y_refs = refs[1:1 + n_y]
    wo_ref, g_ref, wg_ref, wu_ref, wd_ref, o_ref = refs[1 + n_y:]
    x1 = x_ref[...]
    off = 0
    for y_ref in y_refs:
        width = y_ref.shape[1]
        x1 = x1 + _dot(y_ref[...], wo_ref[off:off + width, :])
        off += width
    h = _rms(x1, g_ref[...]).astype(BF16)
    gate = _dot(h, wg_ref[...])
    up = _dot(h, wu_ref[...])
    act = (gate * jax.nn.sigmoid(gate) * up).astype(BF16)
    o_ref[...] = x1 + _dot(act, wd_ref[...])


def _mix_ffn(x2, ys, w_out, norm_g, w_gate, w_up, w_down, *, tm):
    n, d = x2.shape
    f = w_gate.shape[1]
    row = lambda i: (i, 0)
    return pl.pallas_call(
        functools.partial(_mix_ffn_kernel, n_y=len(ys)),
        grid=(n // tm,),
        in_specs=[pl.BlockSpec((tm, d), row)]
                 + [pl.BlockSpec((tm, y.shape[1]), row) for y in ys]
                 + [_const_spec(w_out.shape), _const_spec((1, d)), _const_spec((d, f)),
                    _const_spec((d, f)), _const_spec((f, d))],
        out_specs=pl.BlockSpec((tm, d), row),
        out_shape=jax.ShapeDtypeStruct((n, d), F32),
        compiler_params=pltpu.CompilerParams(dimension_semantics=("arbitrary",),
                                             vmem_limit_bytes=VMEM_LIMIT_BYTES),
        name="outproj_swiglu",
    )(x2, *ys, w_out.astype(BF16), norm_g.reshape(1, d), w_gate.astype(BF16),
      w_up.astype(BF16), w_down.astype(BF16))


def kernel(x, ab_norm_g, ab_w_in, ab_conv_w, ab_conv_b, ab_wa, ab_ba, ab_wx, ab_bx, ab_lru_lambda, ab_q_norm_g, ab_k_norm_g, ab_lambda_q1, ab_lambda_k1, ab_lambda_q2, ab_lambda_k2, ab_subln_g, ab_w_out, c_norm_g, c_w_qkv, c_q_norm_g, c_k_norm_g, c_w_out, ffn_norm_g, ffn_w_gate, ffn_w_up, ffn_w_down):
    bsz, seq_len, d = x.shape
    n = bsz * seq_len
    tables = _rope_tables(seq_len)
    x2 = x.reshape(n, d)

    xg, q, k, v = _project(x2, seq_len, ab_norm_g[0], ab_w_in[0], ab_q_norm_g[0], ab_k_norm_g[0],
                           tables, plain_w=2 * RNN_WIDTH, qk_w=DIFF_QK_WIDTH, v_w=DIFF_WIDTH,
                           tm=512)
    y_rnn = _rglru(xg.reshape(bsz, seq_len, -1), ab_conv_w[0], ab_conv_b[0], ab_wa[0], ab_wx[0],
                   ab_ba[0], ab_bx[0], ab_lru_lambda[0], ts=256)
    to3 = lambda t: t.reshape(bsz, seq_len, -1)
    y_diff = _diff_attention(to3(q), to3(k), to3(v), ab_lambda_q1[0], ab_lambda_k1[0],
                             ab_lambda_q2[0], ab_lambda_k2[0], ab_subln_g[0],
                             layer_idx=0, tq=256, tk=256)
    x2 = _mix_ffn(x2, [y_rnn.reshape(n, -1), y_diff.reshape(n, -1)], ab_w_out[0], ffn_norm_g[0],
                  ffn_w_gate[0], ffn_w_up[0], ffn_w_down[0], tm=256)

    q, k, v = _project(x2, seq_len, c_norm_g[0], c_w_qkv[0], c_q_norm_g[0], c_k_norm_g[0],
                       tables, plain_w=0, qk_w=D_MODEL, v_w=D_MODEL, tm=512)
    score_bound = (HEAD_DIM ** 0.5) * jnp.max(jnp.abs(c_q_norm_g[0])) * jnp.max(jnp.abs(c_k_norm_g[0]))
    o = _dilated_attention(to3(q), to3(k), to3(v), score_bound)
    x2 = _mix_ffn(x2, [o.reshape(n, -1)], c_w_out[0], ffn_norm_g[1], ffn_w_gate[1], ffn_w_up[1],
                  ffn_w_down[1], tm=256)
    return x2.reshape(bsz, seq_len, d)
```

```python
import functools
import math

import jax
import jax.numpy as jnp
from jax import lax
from jax.experimental import pallas as pl
from jax.experimental.pallas import tpu as pltpu

D_MODEL = 1024
NORM_EPS = 1e-6
ROPE_THETA = 500000.0
HEAD_DIM = 64
ROT_DIM = HEAD_DIM // 4
ROT_HALF = ROT_DIM // 2
NEG_INF = -1e30
RNN_WIDTH = D_MODEL // 2
RNN_BLOCKS = 8
RNN_BLOCK = RNN_WIDTH // RNN_BLOCKS
CONV_WIDTH = 4
RGLRU_C = 8.0
DIFF_HEADS = (D_MODEL // 2) // (2 * HEAD_DIM)
DIFF_QK_WIDTH = DIFF_HEADS * 2 * HEAD_DIM
DIFF_V_DIM = 2 * HEAD_DIM
DIFF_WIDTH = DIFF_HEADS * DIFF_V_DIM
DIL_HEADS = D_MODEL // HEAD_DIM
DIL_PATTERNS = ((128, 1), (512, 4), (2048, 16))
DIL_BLOCK = 128
D_FF = -(-8 * D_MODEL // (3 * 256)) * 256

LANES = 128
SUBLANES = 8
MXU_DIM = 256
VMEM_LIMIT_BYTES = 56 * 1024 * 1024

F32 = jnp.float32
BF16 = jnp.bfloat16


def _dot(a, b):
    return jnp.dot(a, b, preferred_element_type=F32)


def _dot_nt(a, b):
    return lax.dot_general(a, b, (((1,), (1,)), ((), ())), preferred_element_type=F32)


def _rms(xf, g):
    ms = jnp.mean(xf * xf, axis=-1, keepdims=True)
    return xf * lax.rsqrt(ms + NORM_EPS) * g


def _const_spec(shape):
    nd = len(shape)
    return pl.BlockSpec(shape, lambda *_: (0,) * nd, pipeline_mode=pl.Buffered(1))


def _head_pair_norm_rope(t, g2, cos_t, sin_a, sin_b):
    lo = lax.broadcasted_iota(jnp.int32, t.shape, 1) < HEAD_DIM
    sq = t * t
    s_lo = jnp.sum(jnp.where(lo, sq, 0.0), axis=-1, keepdims=True)
    s_hi = jnp.sum(jnp.where(lo, 0.0, sq), axis=-1, keepdims=True)
    inv = lax.rsqrt(jnp.where(lo, s_lo, s_hi) * (1.0 / HEAD_DIM) + NORM_EPS)
    y = t * inv * g2
    return (y * cos_t + pltpu.roll(y, ROT_HALF, 1) * sin_a
            + pltpu.roll(y, LANES - ROT_HALF, 1) * sin_b)


def _rope_tables(seq_len):
    pos = jnp.arange(seq_len, dtype=F32)
    inv_freq = 1.0 / (ROPE_THETA ** (jnp.arange(0, ROT_DIM, 2, dtype=F32) / ROT_DIM))
    ang = pos[:, None] * inv_freq[None, :]
    cos, sin = jnp.cos(ang), jnp.sin(ang)
    rest = HEAD_DIM - ROT_DIM
    one_r = jnp.ones((seq_len, rest), F32)
    zero_r = jnp.zeros((seq_len, rest), F32)
    zero_h = jnp.zeros((seq_len, ROT_HALF), F32)
    cos_t = jnp.concatenate([cos, cos, one_r], axis=-1)
    sin_a = jnp.concatenate([zero_h, sin, zero_r], axis=-1)
    sin_b = jnp.concatenate([-sin, zero_h, zero_r], axis=-1)
    pair = lambda t: jnp.concatenate([t, t], axis=-1)
    return pair(cos_t), pair(sin_a), pair(sin_b)


def _proj_kernel(x_ref, g_ref, w_ref, qg_ref, kg_ref, cos_ref, sa_ref, sb_ref, *rest,
                 plain_w, qk_w, v_w, v_transposed):
    h = _rms(x_ref[...], g_ref[...]).astype(BF16)
    if v_transposed:
        wvt_ref, *out_refs = rest
    else:
        out_refs = rest
    if plain_w:
        plain_ref, q_ref, k_ref, v_ref = out_refs
        plain_ref[...] = _dot(h, w_ref[:, 0:plain_w])
    else:
        q_ref, k_ref, v_ref = out_refs
    cos_t, sin_a, sin_b = cos_ref[...], sa_ref[...], sb_ref[...]
    q_scale = HEAD_DIM ** -0.5
    for ref, gref, off, scale in ((q_ref, qg_ref, plain_w, q_scale),
                                  (k_ref, kg_ref, plain_w + qk_w, None)):
        t_all = _dot(h, w_ref[:, off:off + qk_w])
        g2 = gref[...]
        for c in range(qk_w // LANES):
            y = _head_pair_norm_rope(t_all[:, c * LANES:(c + 1) * LANES], g2, cos_t, sin_a, sin_b)
            if scale is not None:
                y = y * scale
            ref[:, c * LANES:(c + 1) * LANES] = y.astype(BF16)
    if v_transposed:
        v_ref[0] = _dot_nt(wvt_ref[...], h).astype(BF16)
    else:
        off = plain_w + 2 * qk_w
        v_ref[...] = _dot(h, w_ref[:, off:off + v_w]).astype(BF16)


def _project(x2, seq_len, norm_g, w, q_gain, k_gain, tables, *, plain_w, qk_w, v_w, tm,
             v_transposed=False):
    n, d = x2.shape
    n_out = plain_w + 2 * qk_w + v_w
    assert w.shape == (d, n_out) and n % tm == 0 and seq_len % tm == 0
    tiles_per_seq = seq_len // tm
    row = lambda i: (i, 0)
    pos = lambda i: (i % tiles_per_seq, 0)
    pair = lambda g: jnp.concatenate([g, g]).reshape(1, LANES).astype(F32)
    out_shape, out_specs = [], []
    if plain_w:
        out_shape.append(jax.ShapeDtypeStruct((n, plain_w), F32))
        out_specs.append(pl.BlockSpec((tm, plain_w), row))
    for width in (qk_w, qk_w):
        out_shape.append(jax.ShapeDtypeStruct((n, width), BF16))
        out_specs.append(pl.BlockSpec((tm, width), row))
    operands = [x2, norm_g.reshape(1, d), w.astype(BF16), pair(q_gain), pair(k_gain), *tables]
    in_specs = [pl.BlockSpec((tm, d), row), _const_spec((1, d)), _const_spec((d, n_out)),
                _const_spec((1, LANES)), _const_spec((1, LANES)),
                pl.BlockSpec((tm, LANES), pos), pl.BlockSpec((tm, LANES), pos),
                pl.BlockSpec((tm, LANES), pos)]
    if v_transposed:
        operands.append(w[:, n_out - v_w:].T.astype(BF16))
        in_specs.append(_const_spec((v_w, d)))
        out_shape.append(jax.ShapeDtypeStruct((n // seq_len, v_w, seq_len), BF16))
        out_specs.append(pl.BlockSpec((1, v_w, tm),
                                      lambda i: (i // tiles_per_seq, 0, i % tiles_per_seq)))
    else:
        out_shape.append(jax.ShapeDtypeStruct((n, v_w), BF16))
        out_specs.append(pl.BlockSpec((tm, v_w), row))
    return pl.pallas_call(
        functools.partial(_proj_kernel, plain_w=plain_w, qk_w=qk_w, v_w=v_w,
                          v_transposed=v_transposed),
        grid=(n // tm,),
        in_specs=in_specs,
        out_specs=out_specs,
        out_shape=out_shape,
        compiler_params=pltpu.CompilerParams(dimension_semantics=("arbitrary",),
                                             vmem_limit_bytes=VMEM_LIMIT_BYTES),
        name="proj_qk_rope",
    )(*operands)


def _rglru_kernel(xr_ref, gate_ref, cw_ref, cb_ref, wax_ref, ba_ref, bx_ref, lam_ref,
                  y_ref, tail_scr, h_scr, *, ts):
    @pl.when(pl.program_id(1) == 0)
    def _():
        tail_scr[...] = jnp.zeros_like(tail_scr)
        h_scr[...] = jnp.zeros_like(h_scr)

    x = xr_ref[0]
    xe = jnp.concatenate([tail_scr[...], x], axis=0)
    cw = cw_ref[...]
    u = cb_ref[...] + x * cw[CONV_WIDTH - 1:CONV_WIDTH]
    for back in range(1, CONV_WIDTH):
        shifted = pltpu.roll(xe, back, 0)[SUBLANES:]
        u = u + shifted * cw[CONV_WIDTH - 1 - back:CONV_WIDTH - back]
    tail_scr[...] = x[ts - SUBLANES:]

    ub = u.astype(BF16)
    half = RNN_WIDTH // 2
    ra0 = _dot(ub[:, :half], wax_ref[0])
    ra1 = _dot(ub[:, half:], wax_ref[1])
    r = jax.nn.sigmoid(jnp.concatenate([ra0[:, :half], ra1[:, :half]], axis=-1) + ba_ref[...])
    i = jax.nn.sigmoid(jnp.concatenate([ra0[:, half:], ra1[:, half:]], axis=-1) + bx_ref[...])
    z = -lam_ref[...]
    softplus = jnp.maximum(z, 0.0) + jnp.log1p(jnp.exp(-jnp.abs(z)))
    log_a = (-RGLRU_C) * r * softplus
    a = jnp.exp(log_a)
    b = jnp.sqrt(-jnp.tanh(log_a) * (a * a + 1.0)) * (i * u)

    rows = lax.broadcasted_iota(jnp.int32, a.shape, 0)
    step = 1
    while step < ts:
        keep = rows >= step
        a_prev = jnp.where(keep, pltpu.roll(a, step, 0), 1.0)
        b_prev = jnp.where(keep, pltpu.roll(b, step, 0), 0.0)
        b = a * b_prev + b
        a = a * a_prev
        step *= 2
    h = b + a * h_scr[0:1, :]
    h_scr[...] = jnp.broadcast_to(h[ts - 1:ts, :], h_scr.shape)
    y_ref[0] = (h * jax.nn.gelu(gate_ref[0])).astype(BF16)


def _rglru(xg3, conv_w, conv_b, wa, wx, ba, bx, lru_lambda, *, ts):
    bsz, seq_len, _ = xg3.shape
    w = RNN_WIDTH
    half = w // 2
    per_half = RNN_BLOCKS // 2

    def block_diag(wt):
        out = jnp.zeros((2, half, half), F32)
        for g in range(RNN_BLOCKS):
            j = (g % per_half) * RNN_BLOCK
            out = out.at[g // per_half, j:j + RNN_BLOCK, j:j + RNN_BLOCK].set(wt[g])
        return out

    wax = jnp.concatenate([block_diag(wa), block_diag(wx)], axis=-1).astype(BF16)
    vec = lambda v: v.reshape(1, w).astype(F32)
    return pl.pallas_call(
        functools.partial(_rglru_kernel, ts=ts),
        grid=(bsz, seq_len // ts),
        in_specs=[pl.BlockSpec((1, ts, w), lambda b, i: (b, i, 0)),
                  pl.BlockSpec((1, ts, w), lambda b, i: (b, i, 1)),
                  _const_spec((CONV_WIDTH, w)), _const_spec((1, w)),
                  _const_spec((2, half, 2 * half)),
                  _const_spec((1, w)), _const_spec((1, w)), _const_spec((1, w))],
        out_specs=pl.BlockSpec((1, ts, w), lambda b, i: (b, i, 0)),
        out_shape=jax.ShapeDtypeStruct((bsz, seq_len, w), BF16),
        scratch_shapes=[pltpu.VMEM((SUBLANES, w), F32), pltpu.VMEM((SUBLANES, w), F32)],
        compiler_params=pltpu.CompilerParams(dimension_semantics=("arbitrary", "arbitrary"),
                                             vmem_limit_bytes=VMEM_LIMIT_BYTES),
        name="rglru",
    )(xg3, xg3, conv_w.astype(F32), vec(conv_b), wax, vec(ba), vec(bx), vec(lru_lambda))


def _diff_attn_kernel(q_ref, k_ref, v_ref, lq1_ref, lk1_ref, lq2_ref, lk2_ref, sg_ref, o_ref,
                      *, tq, tk, lambda_init):
    qi = pl.program_id(2)
    q = q_ref[0]
    lo = lax.broadcasted_iota(jnp.int32, q.shape, 1) < HEAD_DIM
    zero = jnp.zeros_like(q)
    q_parts = (jnp.where(lo, q, zero), jnp.where(lo, zero, q))
    q_pos = qi * tq + lax.broadcasted_iota(jnp.int32, (tq, tk), 0)
    k_off = lax.broadcasted_iota(jnp.int32, (tq, tk), 1)

    def body(j, carry):
        start = pl.multiple_of(j * tk, tk)
        kb = k_ref[0, pl.ds(start, tk), :]
        vb = v_ref[0, pl.ds(start, tk), :]
        visible = (k_off + start) <= q_pos
        new = []
        for c in range(2):
            m, l, acc = carry[c]
            s = jnp.where(visible, _dot_nt(q_parts[c], kb), NEG_INF)
            m_new = jnp.maximum(m, jnp.max(s, axis=-1, keepdims=True))
            alpha = jnp.exp(m - m_new)
            e = jnp.exp(s - m_new)
            l = alpha * l + jnp.sum(e, axis=-1, keepdims=True)
            acc = alpha * acc + _dot(e.astype(BF16), vb)
            new.append((m_new, l, acc))
        return tuple(new)

    init = tuple((jnp.full((tq, 1), NEG_INF, F32), jnp.zeros((tq, 1), F32),
                  jnp.zeros((tq, DIFF_V_DIM), F32)) for _ in range(2))
    n_kv = (qi * tq + tq + tk - 1) // tk
    (_, l0, acc0), (_, l1, acc1) = lax.fori_loop(0, n_kv, body, init)

    dot_l = lambda a, b: jnp.sum(a[...] * b[...], axis=-1, keepdims=True)
    lam = jnp.exp(dot_l(lq1_ref, lk1_ref)) - jnp.exp(dot_l(lq2_ref, lk2_ref)) + lambda_init
    o = acc0 / l0 - lam * (acc1 / l1)
    o_ref[0] = (_rms(o, sg_ref[...]) * (1.0 - lambda_init)).astype(BF16)


def _diff_fast_kernel(q_ref, k_ref, vt_ref, lq1_ref, lk1_ref, lq2_ref, lk2_ref, sg_ref, o_ref,
                      s_scr, *, seq_len, tq, lambda_init):
    dot_l = lambda a, b: jnp.sum(a[...] * b[...], axis=-1, keepdims=True)
    lam = jnp.exp(dot_l(lq1_ref, lk1_ref)) - jnp.exp(dot_l(lq2_ref, lk2_ref)) + lambda_init
    lo = lax.broadcasted_iota(jnp.int32, (tq, LANES), 1) < HEAD_DIM
    key_row = lax.broadcasted_iota(jnp.int32, (tq, 2 * tq), 0)
    q_col = lax.broadcasted_iota(jnp.int32, (tq, 2 * tq), 1) % tq
    causal = key_row <= q_col
    for i in range(seq_len // tq):
        qt = q_ref[0, i * tq:(i + 1) * tq, :]
        zero = jnp.zeros_like(qt)
        q2 = jnp.concatenate([jnp.where(lo, qt, zero), jnp.where(lo, zero, qt)], axis=0)
        n_keys = (i + 1) * tq
        s_scr[0:n_keys, :] = _dot_nt(k_ref[0, 0:n_keys, :], q2)
        acc = jnp.zeros((DIFF_V_DIM, 2 * tq), F32)
        l8 = jnp.zeros((SUBLANES, 2 * tq), F32)
        for j in range(i + 1):
            s = s_scr[j * tq:(j + 1) * tq, :]
            if j == i:
                s = jnp.where(causal, s, NEG_INF)
            e = jnp.exp(s)
            l8 = l8 + jnp.sum(e.reshape(tq // SUBLANES, SUBLANES, 2 * tq), axis=0)
            acc = acc + _dot(vt_ref[0, :, j * tq:(j + 1) * tq], e.astype(BF16))
        l = jnp.sum(l8, axis=0, keepdims=True)
        o = acc[:, :tq] / l[:, :tq] - lam * (acc[:, tq:] / l[:, tq:])
        ms = jnp.mean(o * o, axis=0, keepdims=True)
        y = o * lax.rsqrt(ms + NORM_EPS) * sg_ref[...] * (1.0 - lambda_init)
        o_ref[0, i * tq:(i + 1) * tq, :] = y.T.astype(BF16)


FAST_SCORE_BOUND = 60.0


def _diff_attention(q3, k3, vt3, lq1, lk1, lq2, lk2, subln_g, score_bound, *, layer_idx, tq, tk):
    bsz, seq_len, _ = q3.shape
    lambda_init = 0.8 - 0.6 * math.exp(-0.3 * layer_idx)
    vec = lambda v: v.reshape(1, -1).astype(F32)
    lam_ops = (vec(lq1), vec(lk1), vec(lq2), vec(lk2))
    lam_specs = [_const_spec((1, HEAD_DIM)) for _ in range(4)]
    seq_spec = pl.BlockSpec((1, seq_len, LANES), lambda b, h: (b, 0, h))

    def fast(q3, k3, vt3):
        gain_cols = jnp.broadcast_to(subln_g.astype(F32)[:, None], (DIFF_V_DIM, tq))
        return pl.pallas_call(
            functools.partial(_diff_fast_kernel, seq_len=seq_len, tq=tq, lambda_init=lambda_init),
            grid=(bsz, DIFF_HEADS),
            in_specs=[seq_spec, seq_spec, pl.BlockSpec((1, LANES, seq_len), lambda b, h: (b, h, 0)),
                      *lam_specs, _const_spec((DIFF_V_DIM, tq))],
            out_specs=seq_spec,
            out_shape=jax.ShapeDtypeStruct((bsz, seq_len, DIFF_WIDTH), BF16),
            scratch_shapes=[pltpu.VMEM((seq_len, 2 * tq), F32)],
            compiler_params=pltpu.CompilerParams(dimension_semantics=("arbitrary", "arbitrary"),
                                                 vmem_limit_bytes=VMEM_LIMIT_BYTES),
            name="diff_attn_fast",
        )(q3, k3, vt3, *lam_ops, gain_cols)

    def safe(q3, k3, vt3):
        kv_spec = pl.BlockSpec((1, seq_len, LANES), lambda b, h, i: (b, 0, h))
        return pl.pallas_call(
            functools.partial(_diff_attn_kernel, tq=tq, tk=tk, lambda_init=lambda_init),
            grid=(bsz, DIFF_HEADS, seq_len // tq),
            in_specs=[pl.BlockSpec((1, tq, LANES), lambda b, h, i: (b, i, h)), kv_spec, kv_spec,
                      *lam_specs, _const_spec((1, DIFF_V_DIM))],
            out_specs=pl.BlockSpec((1, tq, LANES), lambda b, h, i: (b, i, h)),
            out_shape=jax.ShapeDtypeStruct((bsz, seq_len, DIFF_WIDTH), BF16),
            compiler_params=pltpu.CompilerParams(
                dimension_semantics=("arbitrary", "arbitrary", "arbitrary"),
                vmem_limit_bytes=VMEM_LIMIT_BYTES),
            name="diff_attn",
        )(q3, k3, jnp.swapaxes(vt3, 1, 2), *lam_ops, vec(subln_g))

    return lax.cond(score_bound <= FAST_SCORE_BOUND, fast, safe, q3, k3, vt3)


def _dil_unit(q0, q1, k2, va, vb, band):
    lo = lax.broadcasted_iota(jnp.int32, (DIL_BLOCK, LANES), 1) < HEAD_DIM
    res, maxes = [], []
    for qh, vh in ((q0, va), (q1, vb)):
        s = jnp.where(band, _dot_nt(qh, k2), NEG_INF)
        m = s[:, :LANES]
        for c in range(1, s.shape[1] // LANES):
            m = jnp.maximum(m, s[:, c * LANES:(c + 1) * LANES])
        m = jnp.max(m, axis=-1, keepdims=True)
        e = jnp.exp(s - m).astype(BF16)
        res.append(_dot(e, vh))
        maxes.append(m)
    acc = jnp.where(lo, res[0], res[1])
    l = pltpu.roll(jnp.where(lo, res[1], res[0]), HEAD_DIM, 1)
    return acc, jnp.where(lo, maxes[0], maxes[1]), l


def _dil_attn_kernel(q_ref, k_ref, v_ref, o_ref, qf, kf, vf, q0c, q1c, kc, vac, vbc,
                     *nat, seq_len):
    blk = DIL_BLOCK
    n_pat = len(DIL_PATTERNS)
    acc_n, m_n, l_n = nat[:n_pat], nat[n_pat:2 * n_pat], nat[2 * n_pat:]
    qf[...] = q_ref[0].astype(F32)
    kf[...] = k_ref[0].astype(F32)
    vf[...] = v_ref[0].astype(F32)

    qi = lax.broadcasted_iota(jnp.int32, (blk, blk), 0)
    ki = lax.broadcasted_iota(jnp.int32, (blk, blk), 1)
    band_cur = ki <= qi
    band_two = jnp.concatenate([ki >= qi, band_cur], axis=1)

    for g, (window, dil) in enumerate(DIL_PATTERNS):
        assert window // dil == blk and seq_len % (blk * dil) == 0
        lc = seq_len // dil
        for r in range(dil):
            rows = pl.ds(r, lc, stride=dil) if dil > 1 else pl.ds(0, lc)
            qv, vv = qf[rows, :], vf[rows, :]
            lo = lax.broadcasted_iota(jnp.int32, qv.shape, 1) < HEAD_DIM
            q0c[0:lc, :] = jnp.where(lo, qv, 0.0).astype(BF16)
            q1c[0:lc, :] = jnp.where(lo, 0.0, qv).astype(BF16)
            kc[0:lc, :] = kf[rows, :].astype(BF16)
            vac[0:lc, :] = jnp.where(lo, vv, 1.0).astype(BF16)
            vbc[0:lc, :] = jnp.where(lo, 1.0, vv).astype(BF16)
            for n in range(lc // blk):
                cur = slice(n * blk, (n + 1) * blk)
                keys = slice(max(n - 1, 0) * blk, (n + 1) * blk)
                acc, m, l = _dil_unit(q0c[cur, :], q1c[cur, :], kc[keys, :], vac[keys, :],
                                      vbc[keys, :], band_two if n else band_cur)
                first = n * blk * dil + r
                dst = pl.ds(first, blk, stride=dil) if dil > 1 else pl.ds(first, blk)
                acc_n[g][dst, :] = acc
                m_n[g][dst, :] = m
                l_n[g][dst, :] = l

    chunk = 256
    for c in range(seq_len // chunk):
        sl = slice(c * chunk, (c + 1) * chunk)
        ms = [m_n[g][sl, :] for g in range(n_pat)]
        m_all = functools.reduce(jnp.maximum, ms)
        num = jnp.zeros((chunk, LANES), F32)
        den = jnp.zeros((chunk, LANES), F32)
        for g in range(n_pat):
            wgt = jnp.exp(ms[g] - m_all)
            num = num + wgt * acc_n[g][sl, :]
            den = den + wgt * l_n[g][sl, :]
        o_ref[0, sl, :] = (num / den).astype(BF16)


def _dil_fast_kernel(q_ref, k_ref, v_ref, o_ref, qf, kf, vf, *bufs, seq_len):
    blk = DIL_BLOCK
    n_pat = len(DIL_PATTERNS)
    cls = [bufs[5 * g:5 * g + 5] for g in range(n_pat)]
    nat = bufs[5 * n_pat:]
    acc_n, l_n = nat[:n_pat], nat[n_pat:]

    qi = lax.broadcasted_iota(jnp.int32, (2 * blk, blk), 0) % blk
    ki = lax.broadcasted_iota(jnp.int32, (2 * blk, blk), 1)
    band_cur = ki <= qi
    band_two = jnp.concatenate([ki >= qi, band_cur], axis=1)
    lo_blk = lax.broadcasted_iota(jnp.int32, (blk, LANES), 1) < HEAD_DIM

    if any(dil > 1 for _, dil in DIL_PATTERNS):
        qf[...] = q_ref[0].astype(F32)
        kf[...] = k_ref[0].astype(F32)
        vf[...] = v_ref[0].astype(F32)

    for g, (window, dil) in enumerate(DIL_PATTERNS):
        assert window // dil == blk and seq_len % (blk * dil) == 0
        lc = seq_len // dil
        q0c, q1c, kc, vac, vbc = cls[g]
        for r in range(dil):
            base = r * lc
            if dil > 1:
                rows = pl.ds(r, lc, stride=dil)
                qv, kv, vv = qf[rows, :], kf[rows, :], vf[rows, :]
            else:
                qv, kv, vv = q_ref[0], k_ref[0], v_ref[0]
            lo = lax.broadcasted_iota(jnp.int32, qv.shape, 1) < HEAD_DIM
            zero, one = jnp.zeros_like(qv), jnp.ones_like(vv)
            q0c[base:base + lc, :] = jnp.where(lo, qv, zero).astype(BF16)
            q1c[base:base + lc, :] = jnp.where(lo, zero, qv).astype(BF16)
            kc[base:base + lc, :] = kv.astype(BF16)
            vac[base:base + lc, :] = jnp.where(lo, vv, one).astype(BF16)
            vbc[base:base + lc, :] = jnp.where(lo, one, vv).astype(BF16)
            for n in range(lc // blk):
                cur = slice(base + n * blk, base + (n + 1) * blk)
                keys = slice(base + max(n - 1, 0) * blk, base + (n + 1) * blk)
                q2 = jnp.concatenate([q0c[cur, :], q1c[cur, :]], axis=0)
                s = _dot_nt(q2, kc[keys, :])
                e = jnp.exp(jnp.where(band_two if n else band_cur, s, NEG_INF)).astype(BF16)
                ra = _dot(e[:blk], vac[keys, :])
                rb = _dot(e[blk:], vbc[keys, :])
                first = n * blk * dil + r
                dst = pl.ds(first, blk, stride=dil) if dil > 1 else pl.ds(first, blk)
                acc_n[g][dst, :] = jnp.where(lo_blk, ra, rb)
                l_n[g][dst, :] = jnp.where(lo_blk, rb, ra)

    chunk = 256
    for c in range(seq_len // chunk):
        sl = slice(c * chunk, (c + 1) * chunk)
        num = functools.reduce(jnp.add, [acc_n[g][sl, :] for g in range(n_pat)])
        den = functools.reduce(jnp.add, [l_n[g][sl, :] for g in range(n_pat)])
        o_ref[0, sl, :] = (num / pltpu.roll(den, HEAD_DIM, 1)).astype(BF16)


def _dilated_attention(q3, k3, v3, score_bound):
    bsz, seq_len, width = q3.shape
    spec = pl.BlockSpec((1, seq_len, LANES), lambda b, h: (b, 0, h))
    n_pat = len(DIL_PATTERNS)
    common = dict(
        grid=(bsz, width // LANES),
        in_specs=[spec, spec, spec],
        out_specs=spec,
        out_shape=jax.ShapeDtypeStruct((bsz, seq_len, width), BF16),
        compiler_params=pltpu.CompilerParams(dimension_semantics=("arbitrary", "arbitrary"),
                                             vmem_limit_bytes=VMEM_LIMIT_BYTES))
    seq_f32 = lambda count: [pltpu.VMEM((seq_len, LANES), F32) for _ in range(count)]
    seq_bf16 = lambda count: [pltpu.VMEM((seq_len, LANES), BF16) for _ in range(count)]
    fast = pl.pallas_call(
        functools.partial(_dil_fast_kernel, seq_len=seq_len),
        scratch_shapes=seq_f32(3) + seq_bf16(5 * n_pat) + seq_f32(2 * n_pat),
        name="dilated_attn_fast", **common)
    safe = pl.pallas_call(
        functools.partial(_dil_attn_kernel, seq_len=seq_len),
        scratch_shapes=seq_f32(3) + seq_bf16(5) + seq_f32(3 * n_pat),
        name="dilated_attn", **common)
    return lax.cond(score_bound <= FAST_SCORE_BOUND, fast, safe, q3, k3, v3)


def _mix_ffn_kernel(*refs, n_y):
    x_ref = refs[0]
    y_refs = refs[1:1 + n_y]
    wo_ref, g_ref, wg_ref, wu_ref, wd_ref, o_ref = refs[1 + n_y:]
    x1 = x_ref[...]
    off = 0
    for y_ref in y_refs:
        width = y_ref.shape[1]
        x1 = x1 + _dot(y_ref[...], wo_ref[off:off + width, :])
        off += width
    h = _rms(x1, g_ref[...]).astype(BF16)
    gate = _dot(h, wg_ref[...])
    up = _dot(h, wu_ref[...])
    act = (gate * jax.nn.sigmoid(gate) * up).astype(BF16)
    o_ref[...] = x1 + _dot(act, wd_ref[...])


def _mix_ffn(x2, ys, w_out, norm_g, w_gate, w_up, w_down, *, tm):
    n, d = x2.shape
    f = w_gate.shape[1]
    row = lambda i: (i, 0)
    return pl.pallas_call(
        functools.partial(_mix_ffn_kernel, n_y=len(ys)),
        grid=(n // tm,),
        in_specs=[pl.BlockSpec((tm, d), row)]
                 + [pl.BlockSpec((tm, y.shape[1]), row) for y in ys]
                 + [_const_spec(w_out.shape), _const_spec((1, d)), _const_spec((d, f)),
                    _const_spec((d, f)), _const_spec((f, d))],
        out_specs=pl.BlockSpec((tm, d), row),
        out_shape=jax.ShapeDtypeStruct((n, d), F32),
        compiler_params=pltpu.CompilerParams(dimension_semantics=("arbitrary",),
                                             vmem_limit_bytes=VMEM_LIMIT_BYTES),
        name="outproj_swiglu",
    )(x2, *ys, w_out.astype(BF16), norm_g.reshape(1, d), w_gate.astype(BF16),
      w_up.astype(BF16), w_down.astype(BF16))


def kernel(x, ab_norm_g, ab_w_in, ab_conv_w, ab_conv_b, ab_wa, ab_ba, ab_wx, ab_bx, ab_lru_lambda, ab_q_norm_g, ab_k_norm_g, ab_lambda_q1, ab_lambda_k1, ab_lambda_q2, ab_lambda_k2, ab_subln_g, ab_w_out, c_norm_g, c_w_qkv, c_q_norm_g, c_k_norm_g, c_w_out, ffn_norm_g, ffn_w_gate, ffn_w_up, ffn_w_down):
    bsz, seq_len, d = x.shape
    n = bsz * seq_len
    tables = _rope_tables(seq_len)
    x2 = x.reshape(n, d)

    xg, q, k, vt = _project(x2, seq_len, ab_norm_g[0], ab_w_in[0], ab_q_norm_g[0], ab_k_norm_g[0],
                            tables, plain_w=2 * RNN_WIDTH, qk_w=DIFF_QK_WIDTH, v_w=DIFF_WIDTH,
                            tm=512, v_transposed=True)
    y_rnn = _rglru(xg.reshape(bsz, seq_len, -1), ab_conv_w[0], ab_conv_b[0], ab_wa[0], ab_wx[0],
                   ab_ba[0], ab_bx[0], ab_lru_lambda[0], ts=256)
    to3 = lambda t: t.reshape(bsz, seq_len, -1)
    score_bound = lambda gq, gk: (HEAD_DIM ** 0.5) * jnp.max(jnp.abs(gq)) * jnp.max(jnp.abs(gk))
    y_diff = _diff_attention(to3(q), to3(k), vt, ab_lambda_q1[0], ab_lambda_k1[0],
                             ab_lambda_q2[0], ab_lambda_k2[0], ab_subln_g[0],
                             score_bound(ab_q_norm_g[0], ab_k_norm_g[0]),
                             layer_idx=0, tq=256, tk=256)
    x2 = _mix_ffn(x2, [y_rnn.reshape(n, -1), y_diff.reshape(n, -1)], ab_w_out[0], ffn_norm_g[0],
                  ffn_w_gate[0], ffn_w_up[0], ffn_w_down[0], tm=256)

    q, k, v = _project(x2, seq_len, c_norm_g[0], c_w_qkv[0], c_q_norm_g[0], c_k_norm_g[0],
                       tables, plain_w=0, qk_w=D_MODEL, v_w=D_MODEL, tm=512)
    o = _dilated_attention(to3(q), to3(k), to3(v), score_bound(c_q_norm_g[0], c_k_norm_g[0]))
    x2 = _mix_ffn(x2, [o.reshape(n, -1)], c_w_out[0], ffn_norm_g[1], ffn_w_gate[1], ffn_w_up[1],
                  ffn_w_down[1], tm=256)
    return x2.reshape(bsz, seq_len, d)
```

```python
import functools
import math

import jax
import jax.numpy as jnp
from jax import lax
from jax.experimental import pallas as pl
from jax.experimental.pallas import tpu as pltpu

D_MODEL = 1024
NORM_EPS = 1e-6
ROPE_THETA = 500000.0
HEAD_DIM = 64
ROT_DIM = HEAD_DIM // 4
ROT_HALF = ROT_DIM // 2
NEG_INF = -1e30
RNN_WIDTH = D_MODEL // 2
RNN_BLOCKS = 8
RNN_BLOCK = RNN_WIDTH // RNN_BLOCKS
CONV_WIDTH = 4
RGLRU_C = 8.0
DIFF_HEADS = (D_MODEL // 2) // (2 * HEAD_DIM)
DIFF_QK_WIDTH = DIFF_HEADS * 2 * HEAD_DIM
DIFF_V_DIM = 2 * HEAD_DIM
DIFF_WIDTH = DIFF_HEADS * DIFF_V_DIM
DIL_HEADS = D_MODEL // HEAD_DIM
DIL_PATTERNS = ((128, 1), (512, 4), (2048, 16))
DIL_BLOCK = 128
D_FF = -(-8 * D_MODEL // (3 * 256)) * 256

LANES = 128
SUBLANES = 8
MXU_DIM = 256
VMEM_LIMIT_BYTES = 56 * 1024 * 1024

F32 = jnp.float32
BF16 = jnp.bfloat16


def _dot(a, b):
    return jnp.dot(a, b, preferred_element_type=F32)


def _dot_nt(a, b):
    return lax.dot_general(a, b, (((1,), (1,)), ((), ())), preferred_element_type=F32)


def _rms(xf, g):
    ms = jnp.mean(xf * xf, axis=-1, keepdims=True)
    return xf * lax.rsqrt(ms + NORM_EPS) * g


def _const_spec(shape):
    nd = len(shape)
    return pl.BlockSpec(shape, lambda *_: (0,) * nd, pipeline_mode=pl.Buffered(1))


def _head_group_norm_rope(t, ones2, tab_c, tab_a, tab_b):
    sq = t * t
    hi = sq.astype(BF16)
    lo = (sq - hi.astype(F32)).astype(BF16)
    ssum = _dot(jnp.concatenate([hi, lo], axis=1), ones2)
    inv = lax.rsqrt(ssum * (1.0 / HEAD_DIM) + NORM_EPS)
    out = []
    for c in range(t.shape[1] // LANES):
        tc = t[:, c * LANES:(c + 1) * LANES]
        out.append(tc * tab_c + pltpu.roll(tc, ROT_HALF, 1) * tab_a
                   + pltpu.roll(tc, LANES - ROT_HALF, 1) * tab_b)
    return inv * jnp.concatenate(out, axis=1)


def _gained_tables(tables, gain, scale):
    cos_t, sin_a, sin_b = tables
    g2 = jnp.concatenate([gain, gain]).reshape(1, LANES).astype(F32)
    return (cos_t * g2 * scale, sin_a * jnp.roll(g2, ROT_HALF, axis=1) * scale,
            sin_b * jnp.roll(g2, -ROT_HALF, axis=1) * scale)


def _rope_tables(seq_len):
    pos = jnp.arange(seq_len, dtype=F32)
    inv_freq = 1.0 / (ROPE_THETA ** (jnp.arange(0, ROT_DIM, 2, dtype=F32) / ROT_DIM))
    ang = pos[:, None] * inv_freq[None, :]
    cos, sin = jnp.cos(ang), jnp.sin(ang)
    rest = HEAD_DIM - ROT_DIM
    one_r = jnp.ones((seq_len, rest), F32)
    zero_r = jnp.zeros((seq_len, rest), F32)
    zero_h = jnp.zeros((seq_len, ROT_HALF), F32)
    cos_t = jnp.concatenate([cos, cos, one_r], axis=-1)
    sin_a = jnp.concatenate([zero_h, sin, zero_r], axis=-1)
    sin_b = jnp.concatenate([-sin, zero_h, zero_r], axis=-1)
    pair = lambda t: jnp.concatenate([t, t], axis=-1)
    return pair(cos_t), pair(sin_a), pair(sin_b)


def _proj_kernel(x_ref, g_ref, w_ref, ones_ref, qc_ref, qa_ref, qb_ref, kc_ref, ka_ref, kb_ref,
                 *rest, plain_w, qk_w, v_w, v_transposed):
    h = _rms(x_ref[...], g_ref[...]).astype(BF16)
    if v_transposed:
        wvt_ref, *out_refs = rest
    else:
        out_refs = rest
    if plain_w:
        plain_ref, q_ref, k_ref, v_ref = out_refs
        plain_ref[...] = _dot(h, w_ref[:, 0:plain_w])
    else:
        q_ref, k_ref, v_ref = out_refs
    ones2 = ones_ref[...]
    per_group = MXU_DIM // LANES
    for ref, tabs, off in ((q_ref, (qc_ref, qa_ref, qb_ref), plain_w),
                           (k_ref, (kc_ref, ka_ref, kb_ref), plain_w + qk_w)):
        t_all = _dot(h, w_ref[:, off:off + qk_w])
        tab_c, tab_a, tab_b = (t[...] for t in tabs)
        for grp in range(qk_w // MXU_DIM):
            y = _head_group_norm_rope(t_all[:, grp * MXU_DIM:(grp + 1) * MXU_DIM], ones2,
                                      tab_c, tab_a, tab_b).astype(BF16)
            for c in range(per_group):
                ref[grp * per_group + c] = y[:, c * LANES:(c + 1) * LANES]
    if v_transposed:
        v_ref[0] = _dot_nt(wvt_ref[...], h).astype(BF16)
    else:
        off = plain_w + 2 * qk_w
        v_all = _dot(h, w_ref[:, off:off + v_w]).astype(BF16)
        for c in range(v_w // LANES):
            v_ref[c] = v_all[:, c * LANES:(c + 1) * LANES]


def _project(x2, seq_len, norm_g, w, q_gain, k_gain, tables, *, plain_w, qk_w, v_w, tm,
             v_transposed=False):
    n, d = x2.shape
    n_out = plain_w + 2 * qk_w + v_w
    assert w.shape == (d, n_out) and n % tm == 0 and seq_len % tm == 0
    tiles_per_seq = seq_len // tm
    row = lambda i: (i, 0)
    pos = lambda i: (i % tiles_per_seq, 0)
    heads_per_group = MXU_DIM // HEAD_DIM
    ones2 = jnp.tile(jnp.kron(jnp.eye(heads_per_group, dtype=F32),
                              jnp.ones((HEAD_DIM, HEAD_DIM), F32)), (2, 1)).astype(BF16)
    rope_in = (*_gained_tables(tables, q_gain, HEAD_DIM ** -0.5),
               *_gained_tables(tables, k_gain, 1.0))
    out_shape, out_specs = [], []
    if plain_w:
        out_shape.append(jax.ShapeDtypeStruct((n, plain_w), F32))
        out_specs.append(pl.BlockSpec((tm, plain_w), row))
    chunked = lambda width: (jax.ShapeDtypeStruct((width // LANES, n, LANES), BF16),
                             pl.BlockSpec((width // LANES, tm, LANES), lambda i: (0, i, 0)))
    for width in (qk_w, qk_w):
        shape, spec = chunked(width)
        out_shape.append(shape)
        out_specs.append(spec)
    operands = [x2, norm_g.reshape(1, d), w.astype(BF16), ones2, *rope_in]
    in_specs = [pl.BlockSpec((tm, d), row), _const_spec((1, d)), _const_spec((d, n_out)),
                _const_spec(ones2.shape)] + [pl.BlockSpec((tm, LANES), pos) for _ in rope_in]
    if v_transposed:
        operands.append(w[:, n_out - v_w:].T.astype(BF16))
        in_specs.append(_const_spec((v_w, d)))
        out_shape.append(jax.ShapeDtypeStruct((n // seq_len, v_w, seq_len), BF16))
        out_specs.append(pl.BlockSpec((1, v_w, tm),
                                      lambda i: (i // tiles_per_seq, 0, i % tiles_per_seq)))
    else:
        shape, spec = chunked(v_w)
        out_shape.append(shape)
        out_specs.append(spec)
    return pl.pallas_call(
        functools.partial(_proj_kernel, plain_w=plain_w, qk_w=qk_w, v_w=v_w,
                          v_transposed=v_transposed),
        grid=(n // tm,),
        in_specs=in_specs,
        out_specs=out_specs,
        out_shape=out_shape,
        compiler_params=pltpu.CompilerParams(dimension_semantics=("arbitrary",),
                                             vmem_limit_bytes=VMEM_LIMIT_BYTES),
        name="proj_qk_rope",
    )(*operands)


def _rglru_kernel(xr_ref, gate_ref, cw_ref, cb_ref, wax_ref, ba_ref, bx_ref, lam_ref,
                  y_ref, tail_scr, h_scr, *, ts):
    @pl.when(pl.program_id(1) == 0)
    def _():
        tail_scr[...] = jnp.zeros_like(tail_scr)
        h_scr[...] = jnp.zeros_like(h_scr)

    x = xr_ref[0]
    xe = jnp.concatenate([tail_scr[...], x], axis=0)
    cw = cw_ref[...]
    u = cb_ref[...] + x * cw[CONV_WIDTH - 1:CONV_WIDTH]
    for back in range(1, CONV_WIDTH):
        shifted = pltpu.roll(xe, back, 0)[SUBLANES:]
        u = u + shifted * cw[CONV_WIDTH - 1 - back:CONV_WIDTH - back]
    tail_scr[...] = x[ts - SUBLANES:]

    ub = u.astype(BF16)
    half = RNN_WIDTH // 2
    ra0 = _dot(ub[:, :half], wax_ref[0])
    ra1 = _dot(ub[:, half:], wax_ref[1])
    r = jax.nn.sigmoid(jnp.concatenate([ra0[:, :half], ra1[:, :half]], axis=-1) + ba_ref[...])
    i = jax.nn.sigmoid(jnp.concatenate([ra0[:, half:], ra1[:, half:]], axis=-1) + bx_ref[...])
    z = -lam_ref[...]
    softplus = jnp.maximum(z, 0.0) + jnp.log1p(jnp.exp(-jnp.abs(z)))
    log_a = (-RGLRU_C) * r * softplus
    a = jnp.exp(log_a)
    b = jnp.sqrt(-jnp.tanh(log_a) * (a * a + 1.0)) * (i * u)

    grouped = (ts // SUBLANES, SUBLANES, a.shape[1])
    a, b = a.reshape(grouped), b.reshape(grouped)
    row_in_group = lax.broadcasted_iota(jnp.int32, grouped, 1)
    step = 1
    while step < SUBLANES:
        keep = row_in_group >= step
        a_prev = jnp.where(keep, pltpu.roll(a, step, 1), 1.0)
        b_prev = jnp.where(keep, pltpu.roll(b, step, 1), 0.0)
        b = a * b_prev + b
        a = a * a_prev
        step *= 2
    a, b = a.reshape(ts, grouped[2]), b.reshape(ts, grouped[2])
    carry = h_scr[0:1, :]
    groups = []
    for grp in range(ts // SUBLANES):
        sl = slice(grp * SUBLANES, (grp + 1) * SUBLANES)
        h_grp = b[sl] + a[sl] * carry
        groups.append(h_grp)
        carry = h_grp[SUBLANES - 1:SUBLANES, :]
    h = jnp.concatenate(groups, axis=0)
    h_scr[...] = jnp.broadcast_to(carry, h_scr.shape)
    y_ref[0] = (h * jax.nn.gelu(gate_ref[0])).astype(BF16)


def _rglru(xg3, conv_w, conv_b, wa, wx, ba, bx, lru_lambda, *, ts):
    bsz, seq_len, _ = xg3.shape
    w = RNN_WIDTH
    half = w // 2
    per_half = RNN_BLOCKS // 2

    def block_diag(wt):
        out = jnp.zeros((2, half, half), F32)
        for g in range(RNN_BLOCKS):
            j = (g % per_half) * RNN_BLOCK
            out = out.at[g // per_half, j:j + RNN_BLOCK, j:j + RNN_BLOCK].set(wt[g])
        return out

    wax = jnp.concatenate([block_diag(wa), block_diag(wx)], axis=-1).astype(BF16)
    vec = lambda v: v.reshape(1, w).astype(F32)
    return pl.pallas_call(
        functools.partial(_rglru_kernel, ts=ts),
        grid=(bsz, seq_len // ts),
        in_specs=[pl.BlockSpec((1, ts, w), lambda b, i: (b, i, 0)),
                  pl.BlockSpec((1, ts, w), lambda b, i: (b, i, 1)),
                  _const_spec((CONV_WIDTH, w)), _const_spec((1, w)),
                  _const_spec((2, half, 2 * half)),
                  _const_spec((1, w)), _const_spec((1, w)), _const_spec((1, w))],
        out_specs=pl.BlockSpec((1, ts, w), lambda b, i: (b, i, 0)),
        out_shape=jax.ShapeDtypeStruct((bsz, seq_len, w), BF16),
        scratch_shapes=[pltpu.VMEM((SUBLANES, w), F32), pltpu.VMEM((SUBLANES, w), F32)],
        compiler_params=pltpu.CompilerParams(dimension_semantics=("arbitrary", "arbitrary"),
                                             vmem_limit_bytes=VMEM_LIMIT_BYTES),
        name="rglru",
    )(xg3, xg3, conv_w.astype(F32), vec(conv_b), wax, vec(ba), vec(bx), vec(lru_lambda))


def _diff_attn_kernel(q_ref, k_ref, v_ref, lq1_ref, lk1_ref, lq2_ref, lk2_ref, sg_ref, o_ref,
                      *, tq, tk, lambda_init):
    qi = pl.program_id(2)
    q = q_ref[...]
    lo = lax.broadcasted_iota(jnp.int32, q.shape, 1) < HEAD_DIM
    zero = jnp.zeros_like(q)
    q_parts = (jnp.where(lo, q, zero), jnp.where(lo, zero, q))
    q_pos = qi * tq + lax.broadcasted_iota(jnp.int32, (tq, tk), 0)
    k_off = lax.broadcasted_iota(jnp.int32, (tq, tk), 1)

    def body(j, carry):
        start = pl.multiple_of(j * tk, tk)
        kb = k_ref[pl.ds(start, tk), :]
        vb = v_ref[pl.ds(start, tk), :]
        visible = (k_off + start) <= q_pos
        new = []
        for c in range(2):
            m, l, acc = carry[c]
            s = jnp.where(visible, _dot_nt(q_parts[c], kb), NEG_INF)
            m_new = jnp.maximum(m, jnp.max(s, axis=-1, keepdims=True))
            alpha = jnp.exp(m - m_new)
            e = jnp.exp(s - m_new)
            l = alpha * l + jnp.sum(e, axis=-1, keepdims=True)
            acc = alpha * acc + _dot(e.astype(BF16), vb)
            new.append((m_new, l, acc))
        return tuple(new)

    init = tuple((jnp.full((tq, 1), NEG_INF, F32), jnp.zeros((tq, 1), F32),
                  jnp.zeros((tq, DIFF_V_DIM), F32)) for _ in range(2))
    n_kv = (qi * tq + tq + tk - 1) // tk
    (_, l0, acc0), (_, l1, acc1) = lax.fori_loop(0, n_kv, body, init)

    dot_l = lambda a, b: jnp.sum(a[...] * b[...], axis=-1, keepdims=True)
    lam = jnp.exp(dot_l(lq1_ref, lk1_ref)) - jnp.exp(dot_l(lq2_ref, lk2_ref)) + lambda_init
    o = acc0 / l0 - lam * (acc1 / l1)
    o_ref[...] = (_rms(o, sg_ref[...]) * (1.0 - lambda_init)).astype(BF16)


def _diff_fast_kernel(q_ref, k_ref, vt_ref, lq1_ref, lk1_ref, lq2_ref, lk2_ref, sg_ref, o_ref,
                      s_scr, *, seq_len, tq, lambda_init):
    dot_l = lambda a, b: jnp.sum(a[...] * b[...], axis=-1, keepdims=True)
    lam = jnp.exp(dot_l(lq1_ref, lk1_ref)) - jnp.exp(dot_l(lq2_ref, lk2_ref)) + lambda_init
    lo = lax.broadcasted_iota(jnp.int32, (tq, LANES), 1) < HEAD_DIM
    key_row = lax.broadcasted_iota(jnp.int32, (tq, 2 * tq), 0)
    q_col = lax.broadcasted_iota(jnp.int32, (tq, 2 * tq), 1) % tq
    causal = key_row <= q_col
    for i in range(seq_len // tq):
        qt = q_ref[i * tq:(i + 1) * tq, :]
        zero = jnp.zeros_like(qt)
        q2 = jnp.concatenate([jnp.where(lo, qt, zero), jnp.where(lo, zero, qt)], axis=0)
        n_keys = (i + 1) * tq
        s_scr[0:n_keys, :] = _dot_nt(k_ref[0:n_keys, :], q2)
        acc = jnp.zeros((DIFF_V_DIM, 2 * tq), F32)
        l8 = jnp.zeros((SUBLANES, 2 * tq), F32)
        for j in range(i + 1):
            s = s_scr[j * tq:(j + 1) * tq, :]
            if j == i:
                s = jnp.where(causal, s, NEG_INF)
            e = jnp.exp(s)
            l8 = l8 + jnp.sum(e.reshape(tq // SUBLANES, SUBLANES, 2 * tq), axis=0)
            acc = acc + _dot(vt_ref[0, :, j * tq:(j + 1) * tq], e.astype(BF16))
        l = jnp.sum(l8, axis=0, keepdims=True)
        o = acc[:, :tq] / l[:, :tq] - lam * (acc[:, tq:] / l[:, tq:])
        ms = jnp.mean(o * o, axis=0, keepdims=True)
        y = o * lax.rsqrt(ms + NORM_EPS) * sg_ref[...] * (1.0 - lambda_init)
        o_ref[i * tq:(i + 1) * tq, :] = y.T.astype(BF16)


FAST_SCORE_BOUND = 60.0


def _diff_attention(q4, k4, vt3, lq1, lk1, lq2, lk2, subln_g, score_bound, *, layer_idx, tq, tk):
    _, bsz, seq_len, _ = q4.shape
    lambda_init = 0.8 - 0.6 * math.exp(-0.3 * layer_idx)
    vec = lambda v: v.reshape(1, -1).astype(F32)
    lam_ops = (vec(lq1), vec(lk1), vec(lq2), vec(lk2))
    lam_specs = [_const_spec((1, HEAD_DIM)) for _ in range(4)]
    seq_spec = pl.BlockSpec((None, None, seq_len, LANES), lambda b, h: (h, b, 0, 0))
    out_shape = jax.ShapeDtypeStruct((DIFF_HEADS, bsz, seq_len, LANES), BF16)

    def fast(q3, k3, vt3):
        gain_cols = jnp.broadcast_to(subln_g.astype(F32)[:, None], (DIFF_V_DIM, tq))
        return pl.pallas_call(
            functools.partial(_diff_fast_kernel, seq_len=seq_len, tq=tq, lambda_init=lambda_init),
            grid=(bsz, DIFF_HEADS),
            in_specs=[seq_spec, seq_spec, pl.BlockSpec((1, LANES, seq_len), lambda b, h: (b, h, 0)),
                      *lam_specs, _const_spec((DIFF_V_DIM, tq))],
            out_specs=seq_spec,
            out_shape=out_shape,
            scratch_shapes=[pltpu.VMEM((seq_len, 2 * tq), F32)],
            compiler_params=pltpu.CompilerParams(dimension_semantics=("arbitrary", "arbitrary"),
                                                 vmem_limit_bytes=VMEM_LIMIT_BYTES),
            name="diff_attn_fast",
        )(q3, k3, vt3, *lam_ops, gain_cols)

    def safe(q3, k3, vt3):
        kv_spec = pl.BlockSpec((None, None, seq_len, LANES), lambda b, h, i: (h, b, 0, 0))
        q_spec = pl.BlockSpec((None, None, tq, LANES), lambda b, h, i: (h, b, i, 0))
        v4 = vt3.reshape(bsz, DIFF_HEADS, DIFF_V_DIM, seq_len).transpose(1, 0, 3, 2)
        return pl.pallas_call(
            functools.partial(_diff_attn_kernel, tq=tq, tk=tk, lambda_init=lambda_init),
            grid=(bsz, DIFF_HEADS, seq_len // tq),
            in_specs=[q_spec, kv_spec, kv_spec, *lam_specs, _const_spec((1, DIFF_V_DIM))],
            out_specs=q_spec,
            out_shape=out_shape,
            compiler_params=pltpu.CompilerParams(
                dimension_semantics=("arbitrary", "arbitrary", "arbitrary"),
                vmem_limit_bytes=VMEM_LIMIT_BYTES),
            name="diff_attn",
        )(q3, k3, v4, *lam_ops, vec(subln_g))

    return lax.cond(score_bound <= FAST_SCORE_BOUND, fast, safe, q4, k4, vt3)


def _dil_unit(q0, q1, k2, va, vb, band):
    lo = lax.broadcasted_iota(jnp.int32, (DIL_BLOCK, LANES), 1) < HEAD_DIM
    res, maxes = [], []
    for qh, vh in ((q0, va), (q1, vb)):
        s = jnp.where(band, _dot_nt(qh, k2), NEG_INF)
        m = s[:, :LANES]
        for c in range(1, s.shape[1] // LANES):
            m = jnp.maximum(m, s[:, c * LANES:(c + 1) * LANES])
        m = jnp.max(m, axis=-1, keepdims=True)
        e = jnp.exp(s - m).astype(BF16)
        res.append(_dot(e, vh))
        maxes.append(m)
    acc = jnp.where(lo, res[0], res[1])
    l = pltpu.roll(jnp.where(lo, res[1], res[0]), HEAD_DIM, 1)
    return acc, jnp.where(lo, maxes[0], maxes[1]), l


def _dil_attn_kernel(q_ref, k_ref, v_ref, o_ref, qf, kf, vf, q0c, q1c, kc, vac, vbc,
                     *nat, seq_len):
    blk = DIL_BLOCK
    n_pat = len(DIL_PATTERNS)
    acc_n, m_n, l_n = nat[:n_pat], nat[n_pat:2 * n_pat], nat[2 * n_pat:]
    qf[...] = q_ref[...].astype(F32)
    kf[...] = k_ref[...].astype(F32)
    vf[...] = v_ref[...].astype(F32)

    qi = lax.broadcasted_iota(jnp.int32, (blk, blk), 0)
    ki = lax.broadcasted_iota(jnp.int32, (blk, blk), 1)
    band_cur = ki <= qi
    band_two = jnp.concatenate([ki >= qi, band_cur], axis=1)

    for g, (window, dil) in enumerate(DIL_PATTERNS):
        assert window // dil == blk and seq_len % (blk * dil) == 0
        lc = seq_len // dil
        for r in range(dil):
            rows = pl.ds(r, lc, stride=dil) if dil > 1 else pl.ds(0, lc)
            qv, vv = qf[rows, :], vf[rows, :]
            lo = lax.broadcasted_iota(jnp.int32, qv.shape, 1) < HEAD_DIM
            q0c[0:lc, :] = jnp.where(lo, qv, 0.0).astype(BF16)
            q1c[0:lc, :] = jnp.where(lo, 0.0, qv).astype(BF16)
            kc[0:lc, :] = kf[rows, :].astype(BF16)
            vac[0:lc, :] = jnp.where(lo, vv, 1.0).astype(BF16)
            vbc[0:lc, :] = jnp.where(lo, 1.0, vv).astype(BF16)
            for n in range(lc // blk):
                cur = slice(n * blk, (n + 1) * blk)
                keys = slice(max(n - 1, 0) * blk, (n + 1) * blk)
                acc, m, l = _dil_unit(q0c[cur, :], q1c[cur, :], kc[keys, :], vac[keys, :],
                                      vbc[keys, :], band_two if n else band_cur)
                first = n * blk * dil + r
                dst = pl.ds(first, blk, stride=dil) if dil > 1 else pl.ds(first, blk)
                acc_n[g][dst, :] = acc
                m_n[g][dst, :] = m
                l_n[g][dst, :] = l

    chunk = 256
    for c in range(seq_len // chunk):
        sl = slice(c * chunk, (c + 1) * chunk)
        ms = [m_n[g][sl, :] for g in range(n_pat)]
        m_all = functools.reduce(jnp.maximum, ms)
        num = jnp.zeros((chunk, LANES), F32)
        den = jnp.zeros((chunk, LANES), F32)
        for g in range(n_pat):
            wgt = jnp.exp(ms[g] - m_all)
            num = num + wgt * acc_n[g][sl, :]
            den = den + wgt * l_n[g][sl, :]
        o_ref[sl, :] = (num / den).astype(BF16)


def _dil_fast_kernel(q_ref, k_ref, v_ref, o_ref, bias_scr, *bufs, seq_len):
    blk = DIL_BLOCK
    n_pat = len(DIL_PATTERNS)
    natural, staged = bufs[0:3], bufs[3:6]
    cls = [bufs[6 + 5 * g:11 + 5 * g] for g in range(n_pat)]
    nat = bufs[6 + 5 * n_pat:]
    acc_n, l_n = nat[:n_pat], nat[n_pat:]

    qi = lax.broadcasted_iota(jnp.int32, (2 * blk, blk), 0) % blk
    ki = lax.broadcasted_iota(jnp.int32, (2 * blk, blk), 1)
    bias_scr[:, 0:blk] = jnp.where(ki >= qi, 0.0, NEG_INF)
    bias_scr[:, blk:] = jnp.where(ki <= qi, 0.0, NEG_INF)
    lo_blk = lax.broadcasted_iota(jnp.int32, (blk, LANES), 1) < HEAD_DIM

    if any(dil > 1 for _, dil in DIL_PATTERNS):
        for dst, src in zip(natural, (q_ref, k_ref, v_ref)):
            dst[...] = src[...].astype(F32)

    staged_dil = 1
    for g, (window, dil) in enumerate(DIL_PATTERNS):
        assert window // dil == blk and seq_len % (blk * dil) == 0
        lc = seq_len // dil
        q0c, q1c, kc, vac, vbc = cls[g]
        refine = staged_dil > 1 and dil % staged_dil == 0
        stage_here = (dil > 1 and not refine
                      and any(d2 > dil and d2 % dil == 0 for _, d2 in DIL_PATTERNS[g + 1:]))
        for r in range(dil):
            base = r * lc
            if dil == 1:
                qv, kv, vv = q_ref[...], k_ref[...], v_ref[...]
            else:
                if refine:
                    first = (r % staged_dil) * (seq_len // staged_dil) + r // staged_dil
                    rows, srcs = pl.ds(first, lc, stride=dil // staged_dil), staged
                else:
                    rows, srcs = pl.ds(r, lc, stride=dil), natural
                qv, kv, vv = (s[rows, :] for s in srcs)
                if stage_here:
                    for dst, val in zip(staged, (qv, kv, vv)):
                        dst[base:base + lc, :] = val
            lo = lax.broadcasted_iota(jnp.int32, qv.shape, 1) < HEAD_DIM
            zero, one = jnp.zeros_like(qv), jnp.ones_like(vv)
            q0c[base:base + lc, :] = jnp.where(lo, qv, zero).astype(BF16)
            q1c[base:base + lc, :] = jnp.where(lo, zero, qv).astype(BF16)
            kc[base:base + lc, :] = kv.astype(BF16)
            vac[base:base + lc, :] = jnp.where(lo, vv, one).astype(BF16)
            vbc[base:base + lc, :] = jnp.where(lo, one, vv).astype(BF16)
            for n in range(lc // blk):
                cur = slice(base + n * blk, base + (n + 1) * blk)
                keys = slice(base + max(n - 1, 0) * blk, base + (n + 1) * blk)
                q2 = jnp.concatenate([q0c[cur, :], q1c[cur, :]], axis=0)
                bias = bias_scr[...] if n else bias_scr[:, blk:]
                e = jnp.exp(_dot_nt(q2, kc[keys, :]) + bias).astype(BF16)
                ra = _dot(e[:blk], vac[keys, :])
                rb = _dot(e[blk:], vbc[keys, :])
                first = n * blk * dil + r
                dst = pl.ds(first, blk, stride=dil) if dil > 1 else pl.ds(first, blk)
                acc_n[g][dst, :] = jnp.where(lo_blk, ra, rb)
                l_n[g][dst, :] = jnp.where(lo_blk, rb, ra)
        if stage_here:
            staged_dil = dil

    chunk = 256
    for c in range(seq_len // chunk):
        sl = slice(c * chunk, (c + 1) * chunk)
        num = functools.reduce(jnp.add, [acc_n[g][sl, :] for g in range(n_pat)])
        den = functools.reduce(jnp.add, [l_n[g][sl, :] for g in range(n_pat)])
        o_ref[sl, :] = (num / pltpu.roll(den, HEAD_DIM, 1)).astype(BF16)


def _dilated_attention(q4, k4, v4, score_bound):
    n_pairs, bsz, seq_len, _ = q4.shape
    spec = pl.BlockSpec((None, None, seq_len, LANES), lambda b, h: (h, b, 0, 0))
    n_pat = len(DIL_PATTERNS)
    common = dict(
        grid=(bsz, n_pairs),
        in_specs=[spec, spec, spec],
        out_specs=spec,
        out_shape=jax.ShapeDtypeStruct(q4.shape, BF16),
        compiler_params=pltpu.CompilerParams(dimension_semantics=("arbitrary", "arbitrary"),
                                             vmem_limit_bytes=VMEM_LIMIT_BYTES))
    seq_f32 = lambda count: [pltpu.VMEM((seq_len, LANES), F32) for _ in range(count)]
    seq_bf16 = lambda count: [pltpu.VMEM((seq_len, LANES), BF16) for _ in range(count)]
    fast = pl.pallas_call(
        functools.partial(_dil_fast_kernel, seq_len=seq_len),
        scratch_shapes=[pltpu.VMEM((2 * DIL_BLOCK, 2 * DIL_BLOCK), F32)]
                       + seq_f32(6) + seq_bf16(5 * n_pat) + seq_f32(2 * n_pat),
        name="dilated_attn_fast", **common)
    safe = pl.pallas_call(
        functools.partial(_dil_attn_kernel, seq_len=seq_len),
        scratch_shapes=seq_f32(3) + seq_bf16(5) + seq_f32(3 * n_pat),
        name="dilated_attn", **common)
    return lax.cond(score_bound <= FAST_SCORE_BOUND, fast, safe, q4, k4, v4)


def _mix_ffn_kernel(*refs, n_y):
    x_ref = refs[0]
    y_refs = refs[1:1 + n_y]
    wo_ref, g_ref, wg_ref, wu_ref, wd_ref, o_ref = refs[1 + n_y:]
    x1 = x_ref[...]
    off = 0
    for y_ref in y_refs:
        if len(y_ref.shape) == 3:
            y = jnp.concatenate([y_ref[c] for c in range(y_ref.shape[0])], axis=-1)
        else:
            y = y_ref[...]
        width = y.shape[1]
        x1 = x1 + _dot(y, wo_ref[off:off + width, :])
        off += width
    h = _rms(x1, g_ref[...]).astype(BF16)
    gate = _dot(h, wg_ref[...])
    up = _dot(h, wu_ref[...])
    act = (gate * jax.nn.sigmoid(gate) * up).astype(BF16)
    o_ref[...] = x1 + _dot(act, wd_ref[...])


def _mix_ffn(x2, ys, w_out, norm_g, w_gate, w_up, w_down, *, tm):
    n, d = x2.shape
    f = w_gate.shape[1]
    row = lambda i: (i, 0)
    return pl.pallas_call(
        functools.partial(_mix_ffn_kernel, n_y=len(ys)),
        grid=(n // tm,),
        in_specs=[pl.BlockSpec((tm, d), row)]
                 + [pl.BlockSpec((tm, y.shape[1]), row) if y.ndim == 2
                    else pl.BlockSpec((y.shape[0], tm, LANES), lambda i: (0, i, 0)) for y in ys]
                 + [_const_spec(w_out.shape), _const_spec((1, d)), _const_spec((d, f)),
                    _const_spec((d, f)), _const_spec((f, d))],
        out_specs=pl.BlockSpec((tm, d), row),
        out_shape=jax.ShapeDtypeStruct((n, d), F32),
        compiler_params=pltpu.CompilerParams(dimension_semantics=("arbitrary",),
                                             vmem_limit_bytes=VMEM_LIMIT_BYTES),
        name="outproj_swiglu",
    )(x2, *ys, w_out.astype(BF16), norm_g.reshape(1, d), w_gate.astype(BF16),
      w_up.astype(BF16), w_down.astype(BF16))


def kernel(x, ab_norm_g, ab_w_in, ab_conv_w, ab_conv_b, ab_wa, ab_ba, ab_wx, ab_bx, ab_lru_lambda, ab_q_norm_g, ab_k_norm_g, ab_lambda_q1, ab_lambda_k1, ab_lambda_q2, ab_lambda_k2, ab_subln_g, ab_w_out, c_norm_g, c_w_qkv, c_q_norm_g, c_k_norm_g, c_w_out, ffn_norm_g, ffn_w_gate, ffn_w_up, ffn_w_down):
    bsz, seq_len, d = x.shape
    n = bsz * seq_len
    tables = _rope_tables(seq_len)
    x2 = x.reshape(n, d)

    xg, q, k, vt = _project(x2, seq_len, ab_norm_g[0], ab_w_in[0], ab_q_norm_g[0], ab_k_norm_g[0],
                            tables, plain_w=2 * RNN_WIDTH, qk_w=DIFF_QK_WIDTH, v_w=DIFF_WIDTH,
                            tm=512, v_transposed=True)
    y_rnn = _rglru(xg.reshape(bsz, seq_len, -1), ab_conv_w[0], ab_conv_b[0], ab_wa[0], ab_wx[0],
                   ab_ba[0], ab_bx[0], ab_lru_lambda[0], ts=256)
    to4 = lambda t: t.reshape(-1, bsz, seq_len, LANES)
    rows = lambda t: t.reshape(-1, n, LANES)
    score_bound = lambda gq, gk: (HEAD_DIM ** 0.5) * jnp.max(jnp.abs(gq)) * jnp.max(jnp.abs(gk))
    y_diff = _diff_attention(to4(q), to4(k), vt, ab_lambda_q1[0], ab_lambda_k1[0],
                             ab_lambda_q2[0], ab_lambda_k2[0], ab_subln_g[0],
                             score_bound(ab_q_norm_g[0], ab_k_norm_g[0]),
                             layer_idx=0, tq=256, tk=256)
    x2 = _mix_ffn(x2, [y_rnn.reshape(n, -1), rows(y_diff)], ab_w_out[0], ffn_norm_g[0],
                  ffn_w_gate[0], ffn_w_up[0], ffn_w_down[0], tm=256)

    q, k, v = _project(x2, seq_len, c_norm_g[0], c_w_qkv[0], c_q_norm_g[0], c_k_norm_g[0],
                       tables, plain_w=0, qk_w=D_MODEL, v_w=D_MODEL, tm=512)
    o = _dilated_attention(to4(q), to4(k), to4(v), score_bound(c_q_norm_g[0], c_k_norm_g[0]))
    x2 = _mix_ffn(x2, [rows(o)], c_w_out[0], ffn_norm_g[1], ffn_w_gate[1], ffn_w_up[1],
                  ffn_w_down[1], tm=256)
    return x2.reshape(bsz, seq_len, d)
```

```python
import functools
import math

import jax
import jax.numpy as jnp
from jax import lax
from jax.experimental import pallas as pl
from jax.experimental.pallas import tpu as pltpu

D_MODEL = 1024
NORM_EPS = 1e-6
ROPE_THETA = 500000.0
HEAD_DIM = 64
ROT_DIM = HEAD_DIM // 4
ROT_HALF = ROT_DIM // 2
NEG_INF = -1e30
RNN_WIDTH = D_MODEL // 2
RNN_BLOCKS = 8
RNN_BLOCK = RNN_WIDTH // RNN_BLOCKS
CONV_WIDTH = 4
RGLRU_C = 8.0
DIFF_HEADS = (D_MODEL // 2) // (2 * HEAD_DIM)
DIFF_QK_WIDTH = DIFF_HEADS * 2 * HEAD_DIM
DIFF_V_DIM = 2 * HEAD_DIM
DIFF_WIDTH = DIFF_HEADS * DIFF_V_DIM
DIL_HEADS = D_MODEL // HEAD_DIM
DIL_PATTERNS = ((128, 1), (512, 4), (2048, 16))
DIL_BLOCK = 128
D_FF = -(-8 * D_MODEL // (3 * 256)) * 256

LANES = 128
SUBLANES = 8
MXU_DIM = 256
VMEM_LIMIT_BYTES = 56 * 1024 * 1024

F32 = jnp.float32
BF16 = jnp.bfloat16


def _dot(a, b):
    return jnp.dot(a, b, preferred_element_type=F32)


def _dot_nt(a, b):
    return lax.dot_general(a, b, (((1,), (1,)), ((), ())), preferred_element_type=F32)


def _rms(xf, g):
    ms = jnp.mean(xf * xf, axis=-1, keepdims=True)
    return xf * lax.rsqrt(ms + NORM_EPS) * g


def _const_spec(shape):
    nd = len(shape)
    return pl.BlockSpec(shape, lambda *_: (0,) * nd, pipeline_mode=pl.Buffered(1))


def _head_group_norm_rope(t, head_ones, tab_c, tab_a, tab_b):
    ssum = _dot((t * t).astype(BF16), head_ones)
    inv = lax.rsqrt(ssum * (1.0 / HEAD_DIM) + NORM_EPS)
    out = []
    for c in range(t.shape[1] // LANES):
        tc = t[:, c * LANES:(c + 1) * LANES]
        out.append(tc * tab_c + pltpu.roll(tc, ROT_HALF, 1) * tab_a
                   + pltpu.roll(tc, LANES - ROT_HALF, 1) * tab_b)
    return inv * jnp.concatenate(out, axis=1)


def _gained_tables(tables, gain, scale):
    cos_t, sin_a, sin_b = tables
    g2 = jnp.concatenate([gain, gain]).reshape(1, LANES).astype(F32)
    return (cos_t * g2 * scale, sin_a * jnp.roll(g2, ROT_HALF, axis=1) * scale,
            sin_b * jnp.roll(g2, -ROT_HALF, axis=1) * scale)


def _rope_tables(seq_len):
    pos = jnp.arange(seq_len, dtype=F32)
    inv_freq = 1.0 / (ROPE_THETA ** (jnp.arange(0, ROT_DIM, 2, dtype=F32) / ROT_DIM))
    ang = pos[:, None] * inv_freq[None, :]
    cos, sin = jnp.cos(ang), jnp.sin(ang)
    rest = HEAD_DIM - ROT_DIM
    one_r = jnp.ones((seq_len, rest), F32)
    zero_r = jnp.zeros((seq_len, rest), F32)
    zero_h = jnp.zeros((seq_len, ROT_HALF), F32)
    cos_t = jnp.concatenate([cos, cos, one_r], axis=-1)
    sin_a = jnp.concatenate([zero_h, sin, zero_r], axis=-1)
    sin_b = jnp.concatenate([-sin, zero_h, zero_r], axis=-1)
    pair = lambda t: jnp.concatenate([t, t], axis=-1)
    return pair(cos_t), pair(sin_a), pair(sin_b)


def _proj_kernel(x_ref, g_ref, w_ref, ones_ref, qc_ref, qa_ref, qb_ref, kc_ref, ka_ref, kb_ref,
                 *rest, plain_w, qk_w, v_w, v_transposed):
    h = _rms(x_ref[...], g_ref[...]).astype(BF16)
    if v_transposed:
        wvt_ref, *out_refs = rest
    else:
        out_refs = rest
    if plain_w:
        plain_ref, q_ref, k_ref, v_ref = out_refs
        plain_ref[...] = _dot(h, w_ref[:, 0:plain_w])
    else:
        q_ref, k_ref, v_ref = out_refs
    head_ones = ones_ref[...]
    per_group = MXU_DIM // LANES
    for ref, tabs, off in ((q_ref, (qc_ref, qa_ref, qb_ref), plain_w),
                           (k_ref, (kc_ref, ka_ref, kb_ref), plain_w + qk_w)):
        t_all = _dot(h, w_ref[:, off:off + qk_w])
        tab_c, tab_a, tab_b = (t[...] for t in tabs)
        for grp in range(qk_w // MXU_DIM):
            y = _head_group_norm_rope(t_all[:, grp * MXU_DIM:(grp + 1) * MXU_DIM], head_ones,
                                      tab_c, tab_a, tab_b).astype(BF16)
            for c in range(per_group):
                ref[grp * per_group + c] = y[:, c * LANES:(c + 1) * LANES]
    if v_transposed:
        v_ref[0] = _dot_nt(wvt_ref[...], h).astype(BF16)
    else:
        off = plain_w + 2 * qk_w
        v_all = _dot(h, w_ref[:, off:off + v_w]).astype(BF16)
        for c in range(v_w // LANES):
            v_ref[c] = v_all[:, c * LANES:(c + 1) * LANES]


def _project(x2, seq_len, norm_g, w, q_gain, k_gain, tables, *, plain_w, qk_w, v_w, tm,
             v_transposed=False):
    n, d = x2.shape
    n_out = plain_w + 2 * qk_w + v_w
    assert w.shape == (d, n_out) and n % tm == 0 and seq_len % tm == 0
    tiles_per_seq = seq_len // tm
    row = lambda i: (i, 0)
    pos = lambda i: (i % tiles_per_seq, 0)
    ones2 = jnp.kron(jnp.eye(MXU_DIM // HEAD_DIM, dtype=F32),
                     jnp.ones((HEAD_DIM, HEAD_DIM), F32)).astype(BF16)
    rope_in = (*_gained_tables(tables, q_gain, HEAD_DIM ** -0.5),
               *_gained_tables(tables, k_gain, 1.0))
    out_shape, out_specs = [], []
    if plain_w:
        out_shape.append(jax.ShapeDtypeStruct((n, plain_w), F32))
        out_specs.append(pl.BlockSpec((tm, plain_w), row))
    chunked = lambda width: (jax.ShapeDtypeStruct((width // LANES, n, LANES), BF16),
                             pl.BlockSpec((width // LANES, tm, LANES), lambda i: (0, i, 0)))
    for width in (qk_w, qk_w):
        shape, spec = chunked(width)
        out_shape.append(shape)
        out_specs.append(spec)
    operands = [x2, norm_g.reshape(1, d), w.astype(BF16), ones2, *rope_in]
    in_specs = [pl.BlockSpec((tm, d), row), _const_spec((1, d)), _const_spec((d, n_out)),
                _const_spec(ones2.shape)] + [pl.BlockSpec((tm, LANES), pos) for _ in rope_in]
    if v_transposed:
        operands.append(w[:, n_out - v_w:].T.astype(BF16))
        in_specs.append(_const_spec((v_w, d)))
        out_shape.append(jax.ShapeDtypeStruct((n // seq_len, v_w, seq_len), BF16))
        out_specs.append(pl.BlockSpec((1, v_w, tm),
                                      lambda i: (i // tiles_per_seq, 0, i % tiles_per_seq)))
    else:
        shape, spec = chunked(v_w)
        out_shape.append(shape)
        out_specs.append(spec)
    return pl.pallas_call(
        functools.partial(_proj_kernel, plain_w=plain_w, qk_w=qk_w, v_w=v_w,
                          v_transposed=v_transposed),
        grid=(n // tm,),
        in_specs=in_specs,
        out_specs=out_specs,
        out_shape=out_shape,
        compiler_params=pltpu.CompilerParams(dimension_semantics=("arbitrary",),
                                             vmem_limit_bytes=VMEM_LIMIT_BYTES),
        name="proj_qk_rope",
    )(*operands)


def _rglru_gates(x, cw_ref, cb_ref, wax_ref, tail_scr):
    ts = x.shape[0]
    xe = jnp.concatenate([tail_scr[...], x], axis=0)
    cw = cw_ref[...]
    u = cb_ref[...] + x * cw[CONV_WIDTH - 1:CONV_WIDTH]
    for back in range(1, CONV_WIDTH):
        shifted = pltpu.roll(xe, back, 0)[SUBLANES:]
        u = u + shifted * cw[CONV_WIDTH - 1 - back:CONV_WIDTH - back]
    tail_scr[...] = x[ts - SUBLANES:]

    ub = u.astype(BF16)
    half = RNN_WIDTH // 2
    ra0 = _dot(ub[:, :half], wax_ref[0])
    ra1 = _dot(ub[:, half:], wax_ref[1])
    pre_r = jnp.concatenate([ra0[:, :half], ra1[:, :half]], axis=-1)
    pre_i = jnp.concatenate([ra0[:, half:], ra1[:, half:]], axis=-1)
    return u, pre_r, pre_i


def _rglru_rows(u, pre_r, pre_i, gate, ba, bx, softplus, carry):
    rows = u.shape[0]
    r = jax.nn.sigmoid(pre_r + ba)
    i = jax.nn.sigmoid(pre_i + bx)
    log_a = (-RGLRU_C) * r * softplus
    a = jnp.exp(log_a)
    b = jnp.sqrt(-jnp.tanh(log_a) * (a * a + 1.0)) * (i * u)

    grouped = (rows // SUBLANES, SUBLANES, a.shape[1])
    a, b = a.reshape(grouped), b.reshape(grouped)
    row_in_group = lax.broadcasted_iota(jnp.int32, grouped, 1)
    step = 1
    while step < SUBLANES:
        keep = row_in_group >= step
        a_prev = jnp.where(keep, pltpu.roll(a, step, 1), 1.0)
        b_prev = jnp.where(keep, pltpu.roll(b, step, 1), 0.0)
        b = a * b_prev + b
        a = a * a_prev
        step *= 2
    a, b = a.reshape(rows, grouped[2]), b.reshape(rows, grouped[2])
    groups = []
    for grp in range(rows // SUBLANES):
        sl = slice(grp * SUBLANES, (grp + 1) * SUBLANES)
        h_grp = b[sl] + a[sl] * carry
        groups.append(h_grp)
        carry = h_grp[SUBLANES - 1:SUBLANES, :]
    h = jnp.concatenate(groups, axis=0)
    return (h * jax.nn.gelu(gate)).astype(BF16), carry


def _rglru_operands(conv_w, conv_b, wa, wx, ba, bx, lru_lambda):
    w = RNN_WIDTH
    half = w // 2
    per_half = RNN_BLOCKS // 2

    def block_diag(wt):
        out = jnp.zeros((2, half, half), F32)
        for g in range(RNN_BLOCKS):
            j = (g % per_half) * RNN_BLOCK
            out = out.at[g // per_half, j:j + RNN_BLOCK, j:j + RNN_BLOCK].set(wt[g])
        return out

    wax = jnp.concatenate([block_diag(wa), block_diag(wx)], axis=-1).astype(BF16)
    vec = lambda v: v.reshape(1, w).astype(F32)
    operands = [conv_w.astype(F32), vec(conv_b), wax, vec(ba), vec(bx), vec(lru_lambda)]
    return operands, [_const_spec(op.shape) for op in operands]


def _diff_attn_kernel(q_ref, k_ref, v_ref, lq1_ref, lk1_ref, lq2_ref, lk2_ref, sg_ref, o_ref,
                      *, tq, tk, lambda_init):
    qi = pl.program_id(2)
    q = q_ref[...]
    lo = lax.broadcasted_iota(jnp.int32, q.shape, 1) < HEAD_DIM
    zero = jnp.zeros_like(q)
    q_parts = (jnp.where(lo, q, zero), jnp.where(lo, zero, q))
    q_pos = qi * tq + lax.broadcasted_iota(jnp.int32, (tq, tk), 0)
    k_off = lax.broadcasted_iota(jnp.int32, (tq, tk), 1)

    def body(j, carry):
        start = pl.multiple_of(j * tk, tk)
        kb = k_ref[pl.ds(start, tk), :]
        vb = v_ref[pl.ds(start, tk), :]
        visible = (k_off + start) <= q_pos
        new = []
        for c in range(2):
            m, l, acc = carry[c]
            s = jnp.where(visible, _dot_nt(q_parts[c], kb), NEG_INF)
            m_new = jnp.maximum(m, jnp.max(s, axis=-1, keepdims=True))
            alpha = jnp.exp(m - m_new)
            e = jnp.exp(s - m_new)
            l = alpha * l + jnp.sum(e, axis=-1, keepdims=True)
            acc = alpha * acc + _dot(e.astype(BF16), vb)
            new.append((m_new, l, acc))
        return tuple(new)

    init = tuple((jnp.full((tq, 1), NEG_INF, F32), jnp.zeros((tq, 1), F32),
                  jnp.zeros((tq, DIFF_V_DIM), F32)) for _ in range(2))
    n_kv = (qi * tq + tq + tk - 1) // tk
    (_, l0, acc0), (_, l1, acc1) = lax.fori_loop(0, n_kv, body, init)

    dot_l = lambda a, b: jnp.sum(a[...] * b[...], axis=-1, keepdims=True)
    lam = jnp.exp(dot_l(lq1_ref, lk1_ref)) - jnp.exp(dot_l(lq2_ref, lk2_ref)) + lambda_init
    o = acc0 / l0 - lam * (acc1 / l1)
    o_ref[...] = (_rms(o, sg_ref[...]) * (1.0 - lambda_init)).astype(BF16)


def _diff_fast_kernel(q_ref, k_ref, vt_ref, lq1_ref, lk1_ref, lq2_ref, lk2_ref, sg_ref, o_ref,
                      s_scr, *, seq_len, tq, lambda_init):
    dot_l = lambda a, b: jnp.sum(a[...] * b[...], axis=-1, keepdims=True)
    lam = jnp.exp(dot_l(lq1_ref, lk1_ref)) - jnp.exp(dot_l(lq2_ref, lk2_ref)) + lambda_init
    lo = lax.broadcasted_iota(jnp.int32, (tq, LANES), 1) < HEAD_DIM
    key_row = lax.broadcasted_iota(jnp.int32, (tq, 2 * tq), 0)
    q_col = lax.broadcasted_iota(jnp.int32, (tq, 2 * tq), 1) % tq
    causal = key_row <= q_col
    for i in range(seq_len // tq):
        qt = q_ref[i * tq:(i + 1) * tq, :]
        zero = jnp.zeros_like(qt)
        q2 = jnp.concatenate([jnp.where(lo, qt, zero), jnp.where(lo, zero, qt)], axis=0)
        n_keys = (i + 1) * tq
        s_scr[0:n_keys, :] = _dot_nt(k_ref[0:n_keys, :], q2)
        acc = jnp.zeros((DIFF_V_DIM, 2 * tq), F32)
        l8 = jnp.zeros((SUBLANES, 2 * tq), F32)
        for j in range(i + 1):
            s = s_scr[j * tq:(j + 1) * tq, :]
            if j == i:
                s = jnp.where(causal, s, NEG_INF)
            e = jnp.exp(s)
            l8 = l8 + jnp.sum(e.reshape(tq // SUBLANES, SUBLANES, 2 * tq), axis=0)
            acc = acc + _dot(vt_ref[0, :, j * tq:(j + 1) * tq], e.astype(BF16))
        l = jnp.sum(l8, axis=0, keepdims=True)
        o = acc[:, :tq] / l[:, :tq] - lam * (acc[:, tq:] / l[:, tq:])
        ms = jnp.mean(o * o, axis=0, keepdims=True)
        y = o * lax.rsqrt(ms + NORM_EPS) * sg_ref[...] * (1.0 - lambda_init)
        o_ref[i * tq:(i + 1) * tq, :] = y.T.astype(BF16)


FAST_SCORE_BOUND = 60.0


def _diff_attention(q4, k4, vt3, lq1, lk1, lq2, lk2, subln_g, score_bound, *, layer_idx, tq, tk):
    _, bsz, seq_len, _ = q4.shape
    lambda_init = 0.8 - 0.6 * math.exp(-0.3 * layer_idx)
    vec = lambda v: v.reshape(1, -1).astype(F32)
    lam_ops = (vec(lq1), vec(lk1), vec(lq2), vec(lk2))
    lam_specs = [_const_spec((1, HEAD_DIM)) for _ in range(4)]
    seq_spec = pl.BlockSpec((None, None, seq_len, LANES), lambda b, h: (h, b, 0, 0))
    out_shape = jax.ShapeDtypeStruct((DIFF_HEADS, bsz, seq_len, LANES), BF16)

    def fast(q3, k3, vt3):
        gain_cols = jnp.broadcast_to(subln_g.astype(F32)[:, None], (DIFF_V_DIM, tq))
        return pl.pallas_call(
            functools.partial(_diff_fast_kernel, seq_len=seq_len, tq=tq, lambda_init=lambda_init),
            grid=(bsz, DIFF_HEADS),
            in_specs=[seq_spec, seq_spec, pl.BlockSpec((1, LANES, seq_len), lambda b, h: (b, h, 0)),
                      *lam_specs, _const_spec((DIFF_V_DIM, tq))],
            out_specs=seq_spec,
            out_shape=out_shape,
            scratch_shapes=[pltpu.VMEM((seq_len, 2 * tq), F32)],
            compiler_params=pltpu.CompilerParams(dimension_semantics=("arbitrary", "arbitrary"),
                                                 vmem_limit_bytes=VMEM_LIMIT_BYTES),
            name="diff_attn_fast",
        )(q3, k3, vt3, *lam_ops, gain_cols)

    def safe(q3, k3, vt3):
        kv_spec = pl.BlockSpec((None, None, seq_len, LANES), lambda b, h, i: (h, b, 0, 0))
        q_spec = pl.BlockSpec((None, None, tq, LANES), lambda b, h, i: (h, b, i, 0))
        v4 = vt3.reshape(bsz, DIFF_HEADS, DIFF_V_DIM, seq_len).transpose(1, 0, 3, 2)
        return pl.pallas_call(
            functools.partial(_diff_attn_kernel, tq=tq, tk=tk, lambda_init=lambda_init),
            grid=(bsz, DIFF_HEADS, seq_len // tq),
            in_specs=[q_spec, kv_spec, kv_spec, *lam_specs, _const_spec((1, DIFF_V_DIM))],
            out_specs=q_spec,
            out_shape=out_shape,
            compiler_params=pltpu.CompilerParams(
                dimension_semantics=("arbitrary", "arbitrary", "arbitrary"),
                vmem_limit_bytes=VMEM_LIMIT_BYTES),
            name="diff_attn",
        )(q3, k3, v4, *lam_ops, vec(subln_g))

    return lax.cond(score_bound <= FAST_SCORE_BOUND, fast, safe, q4, k4, vt3)


def _dil_unit(q0, q1, k2, va, vb, band):
    lo = lax.broadcasted_iota(jnp.int32, (DIL_BLOCK, LANES), 1) < HEAD_DIM
    res, maxes = [], []
    for qh, vh in ((q0, va), (q1, vb)):
        s = jnp.where(band, _dot_nt(qh, k2), NEG_INF)
        m = s[:, :LANES]
        for c in range(1, s.shape[1] // LANES):
            m = jnp.maximum(m, s[:, c * LANES:(c + 1) * LANES])
        m = jnp.max(m, axis=-1, keepdims=True)
        e = jnp.exp(s - m).astype(BF16)
        res.append(_dot(e, vh))
        maxes.append(m)
    acc = jnp.where(lo, res[0], res[1])
    l = pltpu.roll(jnp.where(lo, res[1], res[0]), HEAD_DIM, 1)
    return acc, jnp.where(lo, maxes[0], maxes[1]), l


def _dil_attn_kernel(q_ref, k_ref, v_ref, o_ref, qf, kf, vf, q0c, q1c, kc, vac, vbc,
                     *nat, seq_len):
    blk = DIL_BLOCK
    n_pat = len(DIL_PATTERNS)
    acc_n, m_n, l_n = nat[:n_pat], nat[n_pat:2 * n_pat], nat[2 * n_pat:]
    qf[...] = q_ref[...].astype(F32)
    kf[...] = k_ref[...].astype(F32)
    vf[...] = v_ref[...].astype(F32)

    qi = lax.broadcasted_iota(jnp.int32, (blk, blk), 0)
    ki = lax.broadcasted_iota(jnp.int32, (blk, blk), 1)
    band_cur = ki <= qi
    band_two = jnp.concatenate([ki >= qi, band_cur], axis=1)

    for g, (window, dil) in enumerate(DIL_PATTERNS):
        assert window // dil == blk and seq_len % (blk * dil) == 0
        lc = seq_len // dil
        for r in range(dil):
            rows = pl.ds(r, lc, stride=dil) if dil > 1 else pl.ds(0, lc)
            qv, vv = qf[rows, :], vf[rows, :]
            lo = lax.broadcasted_iota(jnp.int32, qv.shape, 1) < HEAD_DIM
            q0c[0:lc, :] = jnp.where(lo, qv, 0.0).astype(BF16)
            q1c[0:lc, :] = jnp.where(lo, 0.0, qv).astype(BF16)
            kc[0:lc, :] = kf[rows, :].astype(BF16)
            vac[0:lc, :] = jnp.where(lo, vv, 1.0).astype(BF16)
            vbc[0:lc, :] = jnp.where(lo, 1.0, vv).astype(BF16)
            for n in range(lc // blk):
                cur = slice(n * blk, (n + 1) * blk)
                keys = slice(max(n - 1, 0) * blk, (n + 1) * blk)
                acc, m, l = _dil_unit(q0c[cur, :], q1c[cur, :], kc[keys, :], vac[keys, :],
                                      vbc[keys, :], band_two if n else band_cur)
                first = n * blk * dil + r
                dst = pl.ds(first, blk, stride=dil) if dil > 1 else pl.ds(first, blk)
                acc_n[g][dst, :] = acc
                m_n[g][dst, :] = m
                l_n[g][dst, :] = l

    chunk = 256
    for c in range(seq_len // chunk):
        sl = slice(c * chunk, (c + 1) * chunk)
        ms = [m_n[g][sl, :] for g in range(n_pat)]
        m_all = functools.reduce(jnp.maximum, ms)
        num = jnp.zeros((chunk, LANES), F32)
        den = jnp.zeros((chunk, LANES), F32)
        for g in range(n_pat):
            wgt = jnp.exp(ms[g] - m_all)
            num = num + wgt * acc_n[g][sl, :]
            den = den + wgt * l_n[g][sl, :]
        o_ref[sl, :] = (num / den).astype(BF16)


def _dil_fast_kernel(q_ref, k_ref, v_ref, o_ref, bias_scr, *bufs, seq_len):
    blk = DIL_BLOCK
    n_pat = len(DIL_PATTERNS)
    natural, staged = bufs[0:3], bufs[3:6]
    cls = [bufs[6 + 5 * g:11 + 5 * g] for g in range(n_pat)]
    nat = bufs[6 + 5 * n_pat:]
    acc_n, l_n = nat[:n_pat], nat[n_pat:]

    qi = lax.broadcasted_iota(jnp.int32, (2 * blk, blk), 0) % blk
    ki = lax.broadcasted_iota(jnp.int32, (2 * blk, blk), 1)
    bias_scr[:, 0:blk] = jnp.where(ki >= qi, 0.0, NEG_INF)
    bias_scr[:, blk:] = jnp.where(ki <= qi, 0.0, NEG_INF)
    lo_blk = lax.broadcasted_iota(jnp.int32, (blk, LANES), 1) < HEAD_DIM

    if any(dil > 1 for _, dil in DIL_PATTERNS):
        for dst, src in zip(natural, (q_ref, k_ref, v_ref)):
            dst[...] = src[...].astype(F32)

    staged_dil = 1
    for g, (window, dil) in enumerate(DIL_PATTERNS):
        assert window // dil == blk and seq_len % (blk * dil) == 0
        lc = seq_len // dil
        q0c, q1c, kc, vac, vbc = cls[g]
        refine = staged_dil > 1 and dil % staged_dil == 0
        stage_here = (dil > 1 and not refine
                      and any(d2 > dil and d2 % dil == 0 for _, d2 in DIL_PATTERNS[g + 1:]))
        for r in range(dil):
            base = r * lc
            if dil == 1:
                qv, kv, vv = q_ref[...], k_ref[...], v_ref[...]
            else:
                if refine:
                    first = (r % staged_dil) * (seq_len // staged_dil) + r // staged_dil
                    rows, srcs = pl.ds(first, lc, stride=dil // staged_dil), staged
                else:
                    rows, srcs = pl.ds(r, lc, stride=dil), natural
                qv, kv, vv = (s[rows, :] for s in srcs)
                if stage_here:
                    for dst, val in zip(staged, (qv, kv, vv)):
                        dst[base:base + lc, :] = val
            lo = lax.broadcasted_iota(jnp.int32, qv.shape, 1) < HEAD_DIM
            zero, one = jnp.zeros_like(qv), jnp.ones_like(vv)
            q0c[base:base + lc, :] = jnp.where(lo, qv, zero).astype(BF16)
            q1c[base:base + lc, :] = jnp.where(lo, zero, qv).astype(BF16)
            kc[base:base + lc, :] = kv.astype(BF16)
            vac[base:base + lc, :] = jnp.where(lo, vv, one).astype(BF16)
            vbc[base:base + lc, :] = jnp.where(lo, one, vv).astype(BF16)
            for n in range(lc // blk):
                cur = slice(base + n * blk, base + (n + 1) * blk)
                keys = slice(base + max(n - 1, 0) * blk, base + (n + 1) * blk)
                q2 = jnp.concatenate([q0c[cur, :], q1c[cur, :]], axis=0)
                bias = bias_scr[...] if n else bias_scr[:, blk:]
                e = jnp.exp(_dot_nt(q2, kc[keys, :]) + bias).astype(BF16)
                ra = _dot(e[:blk], vac[keys, :])
                rb = _dot(e[blk:], vbc[keys, :])
                first = n * blk * dil + r
                dst = pl.ds(first, blk, stride=dil) if dil > 1 else pl.ds(first, blk)
                acc_n[g][dst, :] = jnp.where(lo_blk, ra, rb)
                l_n[g][dst, :] = jnp.where(lo_blk, rb, ra)
        if stage_here:
            staged_dil = dil

    chunk = 256
    for c in range(seq_len // chunk):
        sl = slice(c * chunk, (c + 1) * chunk)
        num = functools.reduce(jnp.add, [acc_n[g][sl, :] for g in range(n_pat)])
        den = functools.reduce(jnp.add, [l_n[g][sl, :] for g in range(n_pat)])
        o_ref[sl, :] = (num / pltpu.roll(den, HEAD_DIM, 1)).astype(BF16)


def _dilated_attention(q4, k4, v4, score_bound):
    n_pairs, bsz, seq_len, _ = q4.shape
    spec = pl.BlockSpec((None, None, seq_len, LANES), lambda b, h: (h, b, 0, 0))
    n_pat = len(DIL_PATTERNS)
    common = dict(
        grid=(bsz, n_pairs),
        in_specs=[spec, spec, spec],
        out_specs=spec,
        out_shape=jax.ShapeDtypeStruct(q4.shape, BF16),
        compiler_params=pltpu.CompilerParams(dimension_semantics=("arbitrary", "arbitrary"),
                                             vmem_limit_bytes=VMEM_LIMIT_BYTES))
    seq_f32 = lambda count: [pltpu.VMEM((seq_len, LANES), F32) for _ in range(count)]
    seq_bf16 = lambda count: [pltpu.VMEM((seq_len, LANES), BF16) for _ in range(count)]
    fast = pl.pallas_call(
        functools.partial(_dil_fast_kernel, seq_len=seq_len),
        scratch_shapes=[pltpu.VMEM((2 * DIL_BLOCK, 2 * DIL_BLOCK), F32)]
                       + seq_f32(6) + seq_bf16(5 * n_pat) + seq_f32(2 * n_pat),
        name="dilated_attn_fast", **common)
    safe = pl.pallas_call(
        functools.partial(_dil_attn_kernel, seq_len=seq_len),
        scratch_shapes=seq_f32(3) + seq_bf16(5) + seq_f32(3 * n_pat),
        name="dilated_attn", **common)
    return lax.cond(score_bound <= FAST_SCORE_BOUND, fast, safe, q4, k4, v4)


def _mix_ffn_tile(x, ys, wo_ref, g_ref, wg_ref, wu_ref, wd_ref, side_work=()):
    x1 = x
    off = 0
    for y in ys:
        width = y.shape[1]
        x1 = x1 + _dot(y, wo_ref[off:off + width, :])
        off += width
    h = _rms(x1, g_ref[...]).astype(BF16)
    acts = []
    n_chunks = wg_ref.shape[1] // MXU_DIM
    for c in range(n_chunks):
        cols = slice(c * MXU_DIM, (c + 1) * MXU_DIM)
        gate = _dot(h, wg_ref[:, cols])
        up = _dot(h, wu_ref[:, cols])
        acts.append((gate * jax.nn.sigmoid(gate) * up).astype(BF16))
        for work in side_work[c * len(side_work) // n_chunks:(c + 1) * len(side_work) // n_chunks]:
            work(gate[0:1, :])
    return x1 + _dot(jnp.concatenate(acts, axis=-1), wd_ref[...])


def _chunks_to_lanes(y_ref):
    return jnp.concatenate([y_ref[c] for c in range(y_ref.shape[0])], axis=-1)


def _mix_ffn_kernel(x_ref, y_ref, wo_ref, g_ref, wg_ref, wu_ref, wd_ref, o_ref):
    o_ref[...] = _mix_ffn_tile(x_ref[...], [_chunks_to_lanes(y_ref)], wo_ref, g_ref, wg_ref,
                               wu_ref, wd_ref)


RGLRU_PIECE_ROWS = 16


def _rglru_mix_ffn_kernel(x_ref, xr_ref, gate_ref, yd_ref, cw_ref, cb_ref, wax_ref, ba_ref, bx_ref,
                          lam_ref, wo_ref, g_ref, wg_ref, wu_ref, wd_ref, o_ref,
                          tail_scr, h_scr, y_scr, *, tiles_per_seq):
    j = pl.program_id(0)

    @pl.when(j == 0)
    def _():
        y_scr[...] = jnp.zeros_like(y_scr)

    @pl.when(j % tiles_per_seq == 0)
    def _():
        tail_scr[...] = jnp.zeros_like(tail_scr)
        h_scr[...] = jnp.zeros_like(h_scr)

    y_prev = y_scr[...]
    u, pre_r, pre_i = _rglru_gates(xr_ref[...], cw_ref, cb_ref, wax_ref, tail_scr)
    z = -lam_ref[...]
    softplus = jnp.maximum(z, 0.0) + jnp.log1p(jnp.exp(-jnp.abs(z)))
    ba, bx = ba_ref[...], bx_ref[...]
    state = [h_scr[0:1, :]]

    def piece(p):
        def run(anchor):
            floor = jnp.minimum(jnp.concatenate([anchor] * (RNN_WIDTH // anchor.shape[1]), axis=1),
                                -3.0e38)
            sl = slice(p * RGLRU_PIECE_ROWS, (p + 1) * RGLRU_PIECE_ROWS)
            y_rows, state[0] = _rglru_rows(u[sl], jnp.maximum(pre_r[sl], floor),
                                           jnp.maximum(pre_i[sl], floor), gate_ref[sl, :], ba, bx,
                                           softplus, state[0])
            y_scr[sl, :] = y_rows
        return run

    n_pieces = xr_ref.shape[0] // RGLRU_PIECE_ROWS
    o_ref[...] = _mix_ffn_tile(x_ref[...], [y_prev, _chunks_to_lanes(yd_ref)], wo_ref, g_ref,
                               wg_ref, wu_ref, wd_ref,
                               side_work=[piece(p) for p in range(n_pieces)])
    h_scr[...] = jnp.broadcast_to(state[0], h_scr.shape)


def _ffn_operands(w_out, norm_g, w_gate, w_up, w_down):
    d = norm_g.shape[0]
    operands = [w_out.astype(BF16), norm_g.reshape(1, d), w_gate.astype(BF16), w_up.astype(BF16),
                w_down.astype(BF16)]
    return operands, [_const_spec(op.shape) for op in operands]


def _mix_ffn(x2, y, w_out, norm_g, w_gate, w_up, w_down, *, tm):
    n, d = x2.shape
    row = lambda i: (i, 0)
    ffn_ops, ffn_specs = _ffn_operands(w_out, norm_g, w_gate, w_up, w_down)
    return pl.pallas_call(
        _mix_ffn_kernel,
        grid=(n // tm,),
        in_specs=[pl.BlockSpec((tm, d), row),
                  pl.BlockSpec((y.shape[0], tm, LANES), lambda i: (0, i, 0)), *ffn_specs],
        out_specs=pl.BlockSpec((tm, d), row),
        out_shape=jax.ShapeDtypeStruct((n, d), F32),
        compiler_params=pltpu.CompilerParams(dimension_semantics=("arbitrary",),
                                             vmem_limit_bytes=VMEM_LIMIT_BYTES),
        name="outproj_swiglu",
    )(x2, y, *ffn_ops)


def _rglru_mix_ffn(x2, xg, y_diff, seq_len, rglru_params, w_out, norm_g, w_gate, w_up, w_down,
                   *, tm):
    n, d = x2.shape
    w = RNN_WIDTH
    n_tiles = n // tm
    assert seq_len % tm == 0
    prev = lambda j: jnp.maximum(j - 1, 0)
    this = lambda j: jnp.minimum(j, n_tiles - 1)
    rg_ops, rg_specs = _rglru_operands(*rglru_params)
    ffn_ops, ffn_specs = _ffn_operands(w_out, norm_g, w_gate, w_up, w_down)
    return pl.pallas_call(
        functools.partial(_rglru_mix_ffn_kernel, tiles_per_seq=seq_len // tm),
        grid=(n_tiles + 1,),
        in_specs=[pl.BlockSpec((tm, d), lambda j: (prev(j), 0)),
                  pl.BlockSpec((tm, w), lambda j: (this(j), 0)),
                  pl.BlockSpec((tm, w), lambda j: (this(j), 1)),
                  pl.BlockSpec((y_diff.shape[0], tm, LANES), lambda j: (0, prev(j), 0)),
                  *rg_specs, *ffn_specs],
        out_specs=pl.BlockSpec((tm, d), lambda j: (prev(j), 0)),
        out_shape=jax.ShapeDtypeStruct((n, d), F32),
        scratch_shapes=[pltpu.VMEM((SUBLANES, w), F32), pltpu.VMEM((SUBLANES, w), F32),
                        pltpu.VMEM((tm, w), BF16)],
        compiler_params=pltpu.CompilerParams(dimension_semantics=("arbitrary",),
                                             vmem_limit_bytes=VMEM_LIMIT_BYTES),
        name="rglru_outproj_swiglu",
    )(x2, xg, xg, y_diff, *rg_ops, *ffn_ops)


def kernel(x, ab_norm_g, ab_w_in, ab_conv_w, ab_conv_b, ab_wa, ab_ba, ab_wx, ab_bx, ab_lru_lambda, ab_q_norm_g, ab_k_norm_g, ab_lambda_q1, ab_lambda_k1, ab_lambda_q2, ab_lambda_k2, ab_subln_g, ab_w_out, c_norm_g, c_w_qkv, c_q_norm_g, c_k_norm_g, c_w_out, ffn_norm_g, ffn_w_gate, ffn_w_up, ffn_w_down):
    bsz, seq_len, d = x.shape
    n = bsz * seq_len
    tables = _rope_tables(seq_len)
    x2 = x.reshape(n, d)

    xg, q, k, vt = _project(x2, seq_len, ab_norm_g[0], ab_w_in[0], ab_q_norm_g[0], ab_k_norm_g[0],
                            tables, plain_w=2 * RNN_WIDTH, qk_w=DIFF_QK_WIDTH, v_w=DIFF_WIDTH,
                            tm=512, v_transposed=True)
    to4 = lambda t: t.reshape(-1, bsz, seq_len, LANES)
    rows = lambda t: t.reshape(-1, n, LANES)
    score_bound = lambda gq, gk: (HEAD_DIM ** 0.5) * jnp.max(jnp.abs(gq)) * jnp.max(jnp.abs(gk))
    y_diff = _diff_attention(to4(q), to4(k), vt, ab_lambda_q1[0], ab_lambda_k1[0],
                             ab_lambda_q2[0], ab_lambda_k2[0], ab_subln_g[0],
                             score_bound(ab_q_norm_g[0], ab_k_norm_g[0]),
                             layer_idx=0, tq=256, tk=256)
    rglru_params = (ab_conv_w[0], ab_conv_b[0], ab_wa[0], ab_wx[0], ab_ba[0], ab_bx[0],
                    ab_lru_lambda[0])
    x2 = _rglru_mix_ffn(x2, xg, rows(y_diff), seq_len, rglru_params, ab_w_out[0], ffn_norm_g[0],
                        ffn_w_gate[0], ffn_w_up[0], ffn_w_down[0], tm=256)

    q, k, v = _project(x2, seq_len, c_norm_g[0], c_w_qkv[0], c_q_norm_g[0], c_k_norm_g[0],
                       tables, plain_w=0, qk_w=D_MODEL, v_w=D_MODEL, tm=512)
    o = _dilated_attention(to4(q), to4(k), to4(v), score_bound(c_q_norm_g[0], c_k_norm_g[0]))
    x2 = _mix_ffn(x2, rows(o), c_w_out[0], ffn_norm_g[1], ffn_w_gate[1], ffn_w_up[1],
                  ffn_w_down[1], tm=256)
    return x2.reshape(bsz, seq_len, d)
```

```python
import functools
import math

import jax
import jax.numpy as jnp
from jax import lax
from jax.experimental import pallas as pl
from jax.experimental.pallas import tpu as pltpu

D_MODEL = 1024
NORM_EPS = 1e-6
ROPE_THETA = 500000.0
HEAD_DIM = 64
ROT_DIM = HEAD_DIM // 4
ROT_HALF = ROT_DIM // 2
NEG_INF = -1e30
LOG2_E = math.log2(math.e)
RNN_WIDTH = D_MODEL // 2
RNN_BLOCKS = 8
RNN_BLOCK = RNN_WIDTH // RNN_BLOCKS
CONV_WIDTH = 4
RGLRU_C = 8.0
DIFF_HEADS = (D_MODEL // 2) // (2 * HEAD_DIM)
DIFF_QK_WIDTH = DIFF_HEADS * 2 * HEAD_DIM
DIFF_V_DIM = 2 * HEAD_DIM
DIFF_WIDTH = DIFF_HEADS * DIFF_V_DIM
DIL_HEADS = D_MODEL // HEAD_DIM
DIL_PATTERNS = ((128, 1), (512, 4), (2048, 16))
DIL_BLOCK = 128
D_FF = -(-8 * D_MODEL // (3 * 256)) * 256

LANES = 128
SUBLANES = 8
MXU_DIM = 256
VMEM_LIMIT_BYTES = 56 * 1024 * 1024

F32 = jnp.float32
BF16 = jnp.bfloat16


def _dot(a, b):
    return jnp.dot(a, b, preferred_element_type=F32)


def _dot_nt(a, b):
    return lax.dot_general(a, b, (((1,), (1,)), ((), ())), preferred_element_type=F32)


def _rms(xf, g):
    ms = jnp.mean(xf * xf, axis=-1, keepdims=True)
    return xf * lax.rsqrt(ms + NORM_EPS) * g


def _const_spec(shape):
    nd = len(shape)
    return pl.BlockSpec(shape, lambda *_: (0,) * nd, pipeline_mode=pl.Buffered(1))


def _head_group_norm_rope(t, head_mean, tab_c, tab_a, tab_b):
    inv = lax.rsqrt(_dot((t * t).astype(BF16), head_mean) + NORM_EPS)
    out = []
    for c in range(t.shape[1] // LANES):
        tc = t[:, c * LANES:(c + 1) * LANES]
        out.append(tc * tab_c + pltpu.roll(tc, ROT_HALF, 1) * tab_a
                   + pltpu.roll(tc, LANES - ROT_HALF, 1) * tab_b)
    return inv * jnp.concatenate(out, axis=1)


def _gained_tables(tables, gain, scale):
    cos_t, sin_a, sin_b = tables
    g2 = jnp.concatenate([gain, gain]).reshape(1, LANES).astype(F32)
    return (cos_t * g2 * scale, sin_a * jnp.roll(g2, ROT_HALF, axis=1) * scale,
            sin_b * jnp.roll(g2, -ROT_HALF, axis=1) * scale)


def _rope_tables(seq_len):
    pos = jnp.arange(seq_len, dtype=F32)
    inv_freq = 1.0 / (ROPE_THETA ** (jnp.arange(0, ROT_DIM, 2, dtype=F32) / ROT_DIM))
    ang = pos[:, None] * inv_freq[None, :]
    cos, sin = jnp.cos(ang), jnp.sin(ang)
    rest = HEAD_DIM - ROT_DIM
    one_r = jnp.ones((seq_len, rest), F32)
    zero_r = jnp.zeros((seq_len, rest), F32)
    zero_h = jnp.zeros((seq_len, ROT_HALF), F32)
    cos_t = jnp.concatenate([cos, cos, one_r], axis=-1)
    sin_a = jnp.concatenate([zero_h, sin, zero_r], axis=-1)
    sin_b = jnp.concatenate([-sin, zero_h, zero_r], axis=-1)
    pair = lambda t: jnp.concatenate([t, t], axis=-1)
    return pair(cos_t), pair(sin_a), pair(sin_b)


def _proj_kernel(x_ref, g_ref, w_ref, ones_ref, qc_ref, qa_ref, qb_ref, kc_ref, ka_ref, kb_ref,
                 *rest, plain_w, qk_w, v_w, v_transposed):
    h = _rms(x_ref[...], g_ref[...]).astype(BF16)
    if v_transposed:
        wvt_ref, *out_refs = rest
    else:
        out_refs = rest
    if plain_w:
        plain_ref, q_ref, k_ref, v_ref = out_refs
        plain_ref[...] = _dot(h, w_ref[:, 0:plain_w])
    else:
        q_ref, k_ref, v_ref = out_refs
    head_mean = ones_ref[...]
    per_group = MXU_DIM // LANES
    for ref, tabs, off in ((q_ref, (qc_ref, qa_ref, qb_ref), plain_w),
                           (k_ref, (kc_ref, ka_ref, kb_ref), plain_w + qk_w)):
        t_all = _dot(h, w_ref[:, off:off + qk_w])
        tab_c, tab_a, tab_b = (t[...] for t in tabs)
        for grp in range(qk_w // MXU_DIM):
            y = _head_group_norm_rope(t_all[:, grp * MXU_DIM:(grp + 1) * MXU_DIM], head_mean,
                                      tab_c, tab_a, tab_b).astype(BF16)
            for c in range(per_group):
                ref[grp * per_group + c] = y[:, c * LANES:(c + 1) * LANES]
    if v_transposed:
        v_ref[0] = _dot_nt(wvt_ref[...], h).astype(BF16)
    else:
        off = plain_w + 2 * qk_w
        v_all = _dot(h, w_ref[:, off:off + v_w]).astype(BF16)
        for c in range(v_w // LANES):
            v_ref[c] = v_all[:, c * LANES:(c + 1) * LANES]


def _project(x2, seq_len, norm_g, w, q_gain, k_gain, tables, *, plain_w, qk_w, v_w, tm,
             v_transposed=False):
    n, d = x2.shape
    n_out = plain_w + 2 * qk_w + v_w
    assert w.shape == (d, n_out) and n % tm == 0 and seq_len % tm == 0
    tiles_per_seq = seq_len // tm
    row = lambda i: (i, 0)
    pos = lambda i: (i % tiles_per_seq, 0)
    ones2 = jnp.kron(jnp.eye(MXU_DIM // HEAD_DIM, dtype=F32),
                     jnp.full((HEAD_DIM, HEAD_DIM), 1.0 / HEAD_DIM, F32)).astype(BF16)
    rope_in = (*_gained_tables(tables, q_gain, HEAD_DIM ** -0.5 * LOG2_E),
               *_gained_tables(tables, k_gain, 1.0))
    out_shape, out_specs = [], []
    if plain_w:
        out_shape.append(jax.ShapeDtypeStruct((n, plain_w), F32))
        out_specs.append(pl.BlockSpec((tm, plain_w), row))
    chunked = lambda width: (jax.ShapeDtypeStruct((width // LANES, n, LANES), BF16),
                             pl.BlockSpec((width // LANES, tm, LANES), lambda i: (0, i, 0)))
    for width in (qk_w, qk_w):
        shape, spec = chunked(width)
        out_shape.append(shape)
        out_specs.append(spec)
    operands = [x2, norm_g.reshape(1, d), w.astype(BF16), ones2, *rope_in]
    in_specs = [pl.BlockSpec((tm, d), row), _const_spec((1, d)), _const_spec((d, n_out)),
                _const_spec(ones2.shape)] + [pl.BlockSpec((tm, LANES), pos) for _ in rope_in]
    if v_transposed:
        operands.append(w[:, n_out - v_w:].T.astype(BF16))
        in_specs.append(_const_spec((v_w, d)))
        out_shape.append(jax.ShapeDtypeStruct((n // seq_len, v_w, seq_len), BF16))
        out_specs.append(pl.BlockSpec((1, v_w, tm),
                                      lambda i: (i // tiles_per_seq, 0, i % tiles_per_seq)))
    else:
        shape, spec = chunked(v_w)
        out_shape.append(shape)
        out_specs.append(spec)
    return pl.pallas_call(
        functools.partial(_proj_kernel, plain_w=plain_w, qk_w=qk_w, v_w=v_w,
                          v_transposed=v_transposed),
        grid=(n // tm,),
        in_specs=in_specs,
        out_specs=out_specs,
        out_shape=out_shape,
        compiler_params=pltpu.CompilerParams(dimension_semantics=("arbitrary",),
                                             vmem_limit_bytes=VMEM_LIMIT_BYTES),
        name="proj_qk_rope",
    )(*operands)


def _rglru_gates(x, cw_ref, cb_ref, wax_ref, tail_scr):
    ts = x.shape[0]
    xe = jnp.concatenate([tail_scr[...], x], axis=0)
    cw = cw_ref[...]
    u = cb_ref[...] + x * cw[CONV_WIDTH - 1:CONV_WIDTH]
    for back in range(1, CONV_WIDTH):
        shifted = pltpu.roll(xe, back, 0)[SUBLANES:]
        u = u + shifted * cw[CONV_WIDTH - 1 - back:CONV_WIDTH - back]
    tail_scr[...] = x[ts - SUBLANES:]

    ub = u.astype(BF16)
    half = RNN_WIDTH // 2
    ra0 = _dot(ub[:, :half], wax_ref[0])
    ra1 = _dot(ub[:, half:], wax_ref[1])
    pre_r = jnp.concatenate([ra0[:, :half], ra1[:, :half]], axis=-1)
    pre_i = jnp.concatenate([ra0[:, half:], ra1[:, half:]], axis=-1)
    return u, pre_r, pre_i


def _rglru_rows(u, pre_r, pre_i, gate, ba, bx, softplus, carry):
    rows = u.shape[0]
    r = jax.nn.sigmoid(pre_r + ba)
    i = jax.nn.sigmoid(pre_i + bx)
    log_a = (-RGLRU_C) * r * softplus
    a = jnp.exp(log_a)
    b = jnp.sqrt(-jnp.tanh(log_a) * (a * a + 1.0)) * (i * u)

    grouped = (rows // SUBLANES, SUBLANES, a.shape[1])
    a, b = a.reshape(grouped), b.reshape(grouped)
    row_in_group = lax.broadcasted_iota(jnp.int32, grouped, 1)
    step = 1
    while step < SUBLANES:
        keep = row_in_group >= step
        a_prev = jnp.where(keep, pltpu.roll(a, step, 1), 1.0)
        b_prev = jnp.where(keep, pltpu.roll(b, step, 1), 0.0)
        b = a * b_prev + b
        a = a * a_prev
        step *= 2
    a, b = a.reshape(rows, grouped[2]), b.reshape(rows, grouped[2])
    groups = []
    for grp in range(rows // SUBLANES):
        sl = slice(grp * SUBLANES, (grp + 1) * SUBLANES)
        h_grp = b[sl] + a[sl] * carry
        groups.append(h_grp)
        carry = h_grp[SUBLANES - 1:SUBLANES, :]
    h = jnp.concatenate(groups, axis=0)
    return (h * jax.nn.gelu(gate)).astype(BF16), carry


def _rglru_operands(conv_w, conv_b, wa, wx, ba, bx, lru_lambda):
    w = RNN_WIDTH
    half = w // 2
    per_half = RNN_BLOCKS // 2

    def block_diag(wt):
        out = jnp.zeros((2, half, half), F32)
        for g in range(RNN_BLOCKS):
            j = (g % per_half) * RNN_BLOCK
            out = out.at[g // per_half, j:j + RNN_BLOCK, j:j + RNN_BLOCK].set(wt[g])
        return out

    wax = jnp.concatenate([block_diag(wa), block_diag(wx)], axis=-1).astype(BF16)
    vec = lambda v: v.reshape(1, w).astype(F32)
    operands = [conv_w.astype(F32), vec(conv_b), wax, vec(ba), vec(bx), vec(lru_lambda)]
    return operands, [_const_spec(op.shape) for op in operands]


def _diff_attn_kernel(q_ref, k_ref, v_ref, lq1_ref, lk1_ref, lq2_ref, lk2_ref, sg_ref, o_ref,
                      *, tq, tk, lambda_init):
    qi = pl.program_id(2)
    q = q_ref[...]
    lo = lax.broadcasted_iota(jnp.int32, q.shape, 1) < HEAD_DIM
    zero = jnp.zeros_like(q)
    q_parts = (jnp.where(lo, q, zero), jnp.where(lo, zero, q))
    q_pos = qi * tq + lax.broadcasted_iota(jnp.int32, (tq, tk), 0)
    k_off = lax.broadcasted_iota(jnp.int32, (tq, tk), 1)

    def body(j, carry):
        start = pl.multiple_of(j * tk, tk)
        kb = k_ref[pl.ds(start, tk), :]
        vb = v_ref[pl.ds(start, tk), :]
        visible = (k_off + start) <= q_pos
        new = []
        for c in range(2):
            m, l, acc = carry[c]
            s = jnp.where(visible, _dot_nt(q_parts[c], kb), NEG_INF)
            m_new = jnp.maximum(m, jnp.max(s, axis=-1, keepdims=True))
            alpha = jnp.exp2(m - m_new)
            e = jnp.exp2(s - m_new)
            l = alpha * l + jnp.sum(e, axis=-1, keepdims=True)
            acc = alpha * acc + _dot(e.astype(BF16), vb)
            new.append((m_new, l, acc))
        return tuple(new)

    init = tuple((jnp.full((tq, 1), NEG_INF, F32), jnp.zeros((tq, 1), F32),
                  jnp.zeros((tq, DIFF_V_DIM), F32)) for _ in range(2))
    n_kv = (qi * tq + tq + tk - 1) // tk
    (_, l0, acc0), (_, l1, acc1) = lax.fori_loop(0, n_kv, body, init)

    dot_l = lambda a, b: jnp.sum(a[...] * b[...], axis=-1, keepdims=True)
    lam = jnp.exp(dot_l(lq1_ref, lk1_ref)) - jnp.exp(dot_l(lq2_ref, lk2_ref)) + lambda_init
    o = acc0 / l0 - lam * (acc1 / l1)
    o_ref[...] = (_rms(o, sg_ref[...]) * (1.0 - lambda_init)).astype(BF16)


def _diff_fast_kernel(q_ref, k_ref, vt_ref, lq1_ref, lk1_ref, lq2_ref, lk2_ref, sg_ref, o_ref,
                      s_scr, *, seq_len, tq, lambda_init):
    dot_l = lambda a, b: jnp.sum(a[...] * b[...], axis=-1, keepdims=True)
    lam = jnp.exp(dot_l(lq1_ref, lk1_ref)) - jnp.exp(dot_l(lq2_ref, lk2_ref)) + lambda_init
    lo = lax.broadcasted_iota(jnp.int32, (tq, LANES), 1) < HEAD_DIM
    key_row = lax.broadcasted_iota(jnp.int32, (tq, 2 * tq), 0)
    q_col = lax.broadcasted_iota(jnp.int32, (tq, 2 * tq), 1) % tq
    causal = key_row <= q_col
    for i in range(seq_len // tq):
        qt = q_ref[i * tq:(i + 1) * tq, :]
        zero = jnp.zeros_like(qt)
        q2 = jnp.concatenate([jnp.where(lo, qt, zero), jnp.where(lo, zero, qt)], axis=0)
        n_keys = (i + 1) * tq
        s_scr[0:n_keys, :] = _dot_nt(k_ref[0:n_keys, :], q2)
        acc = jnp.zeros((DIFF_V_DIM, 2 * tq), F32)
        l8 = jnp.zeros((SUBLANES, 2 * tq), F32)
        for j in range(i + 1):
            s = s_scr[j * tq:(j + 1) * tq, :]
            if j == i:
                s = jnp.where(causal, s, NEG_INF)
            e = jnp.exp2(s)
            l8 = l8 + jnp.sum(e.reshape(tq // SUBLANES, SUBLANES, 2 * tq), axis=0)
            acc = acc + _dot(vt_ref[0, :, j * tq:(j + 1) * tq], e.astype(BF16))
        l = jnp.sum(l8, axis=0, keepdims=True)
        o = acc[:, :tq] / l[:, :tq] - lam * (acc[:, tq:] / l[:, tq:])
        ms = jnp.mean(o * o, axis=0, keepdims=True)
        y = o * lax.rsqrt(ms + NORM_EPS) * sg_ref[...] * (1.0 - lambda_init)
        o_ref[i * tq:(i + 1) * tq, :] = y.T.astype(BF16)


FAST_SCORE_BOUND = 60.0


def _diff_attention(q4, k4, vt3, lq1, lk1, lq2, lk2, subln_g, score_bound, *, layer_idx, tq, tk):
    _, bsz, seq_len, _ = q4.shape
    lambda_init = 0.8 - 0.6 * math.exp(-0.3 * layer_idx)
    vec = lambda v: v.reshape(1, -1).astype(F32)
    lam_ops = (vec(lq1), vec(lk1), vec(lq2), vec(lk2))
    lam_specs = [_const_spec((1, HEAD_DIM)) for _ in range(4)]
    seq_spec = pl.BlockSpec((None, None, seq_len, LANES), lambda b, h: (h, b, 0, 0))
    out_shape = jax.ShapeDtypeStruct((DIFF_HEADS, bsz, seq_len, LANES), BF16)

    def fast(q3, k3, vt3):
        gain_cols = jnp.broadcast_to(subln_g.astype(F32)[:, None], (DIFF_V_DIM, tq))
        return pl.pallas_call(
            functools.partial(_diff_fast_kernel, seq_len=seq_len, tq=tq, lambda_init=lambda_init),
            grid=(bsz, DIFF_HEADS),
            in_specs=[seq_spec, seq_spec, pl.BlockSpec((1, LANES, seq_len), lambda b, h: (b, h, 0)),
                      *lam_specs, _const_spec((DIFF_V_DIM, tq))],
            out_specs=seq_spec,
            out_shape=out_shape,
            scratch_shapes=[pltpu.VMEM((seq_len, 2 * tq), F32)],
            compiler_params=pltpu.CompilerParams(dimension_semantics=("arbitrary", "arbitrary"),
                                                 vmem_limit_bytes=VMEM_LIMIT_BYTES),
            name="diff_attn_fast",
        )(q3, k3, vt3, *lam_ops, gain_cols)

    def safe(q3, k3, vt3):
        kv_spec = pl.BlockSpec((None, None, seq_len, LANES), lambda b, h, i: (h, b, 0, 0))
        q_spec = pl.BlockSpec((None, None, tq, LANES), lambda b, h, i: (h, b, i, 0))
        v4 = vt3.reshape(bsz, DIFF_HEADS, DIFF_V_DIM, seq_len).transpose(1, 0, 3, 2)
        return pl.pallas_call(
            functools.partial(_diff_attn_kernel, tq=tq, tk=tk, lambda_init=lambda_init),
            grid=(bsz, DIFF_HEADS, seq_len // tq),
            in_specs=[q_spec, kv_spec, kv_spec, *lam_specs, _const_spec((1, DIFF_V_DIM))],
            out_specs=q_spec,
            out_shape=out_shape,
            compiler_params=pltpu.CompilerParams(
                dimension_semantics=("arbitrary", "arbitrary", "arbitrary"),
                vmem_limit_bytes=VMEM_LIMIT_BYTES),
            name="diff_attn",
        )(q3, k3, v4, *lam_ops, vec(subln_g))

    return lax.cond(score_bound <= FAST_SCORE_BOUND, fast, safe, q4, k4, vt3)


def _dil_unit(q0, q1, k2, va, vb, band):
    lo = lax.broadcasted_iota(jnp.int32, (DIL_BLOCK, LANES), 1) < HEAD_DIM
    res, maxes = [], []
    for qh, vh in ((q0, va), (q1, vb)):
        s = jnp.where(band, _dot_nt(qh, k2), NEG_INF)
        m = s[:, :LANES]
        for c in range(1, s.shape[1] // LANES):
            m = jnp.maximum(m, s[:, c * LANES:(c + 1) * LANES])
        m = jnp.max(m, axis=-1, keepdims=True)
        e = jnp.exp2(s - m).astype(BF16)
        res.append(_dot(e, vh))
        maxes.append(m)
    acc = jnp.where(lo, res[0], res[1])
    l = pltpu.roll(jnp.where(lo, res[1], res[0]), HEAD_DIM, 1)
    return acc, jnp.where(lo, maxes[0], maxes[1]), l


def _dil_attn_kernel(q_ref, k_ref, v_ref, o_ref, qf, kf, vf, q0c, q1c, kc, vac, vbc,
                     *nat, seq_len):
    blk = DIL_BLOCK
    n_pat = len(DIL_PATTERNS)
    acc_n, m_n, l_n = nat[:n_pat], nat[n_pat:2 * n_pat], nat[2 * n_pat:]
    qf[...] = q_ref[...].astype(F32)
    kf[...] = k_ref[...].astype(F32)
    vf[...] = v_ref[...].astype(F32)

    qi = lax.broadcasted_iota(jnp.int32, (blk, blk), 0)
    ki = lax.broadcasted_iota(jnp.int32, (blk, blk), 1)
    band_cur = ki <= qi
    band_two = jnp.concatenate([ki >= qi, band_cur], axis=1)

    for g, (window, dil) in enumerate(DIL_PATTERNS):
        assert window // dil == blk and seq_len % (blk * dil) == 0
        lc = seq_len // dil
        for r in range(dil):
            rows = pl.ds(r, lc, stride=dil) if dil > 1 else pl.ds(0, lc)
            qv, vv = qf[rows, :], vf[rows, :]
            lo = lax.broadcasted_iota(jnp.int32, qv.shape, 1) < HEAD_DIM
            q0c[0:lc, :] = jnp.where(lo, qv, 0.0).astype(BF16)
            q1c[0:lc, :] = jnp.where(lo, 0.0, qv).astype(BF16)
            kc[0:lc, :] = kf[rows, :].astype(BF16)
            vac[0:lc, :] = jnp.where(lo, vv, 1.0).astype(BF16)
            vbc[0:lc, :] = jnp.where(lo, 1.0, vv).astype(BF16)
            for n in range(lc // blk):
                cur = slice(n * blk, (n + 1) * blk)
                keys = slice(max(n - 1, 0) * blk, (n + 1) * blk)
                acc, m, l = _dil_unit(q0c[cur, :], q1c[cur, :], kc[keys, :], vac[keys, :],
                                      vbc[keys, :], band_two if n else band_cur)
                first = n * blk * dil + r
                dst = pl.ds(first, blk, stride=dil) if dil > 1 else pl.ds(first, blk)
                acc_n[g][dst, :] = acc
                m_n[g][dst, :] = m
                l_n[g][dst, :] = l

    chunk = 256
    for c in range(seq_len // chunk):
        sl = slice(c * chunk, (c + 1) * chunk)
        ms = [m_n[g][sl, :] for g in range(n_pat)]
        m_all = functools.reduce(jnp.maximum, ms)
        num = jnp.zeros((chunk, LANES), F32)
        den = jnp.zeros((chunk, LANES), F32)
        for g in range(n_pat):
            wgt = jnp.exp2(ms[g] - m_all)
            num = num + wgt * acc_n[g][sl, :]
            den = den + wgt * l_n[g][sl, :]
        o_ref[sl, :] = (num / den).astype(BF16)


def _dil_fast_kernel(q_ref, k_ref, v_ref, o_ref, bias_scr, *bufs, seq_len):
    blk = DIL_BLOCK
    n_pat = len(DIL_PATTERNS)
    natural, staged = bufs[0:3], bufs[3:6]
    out_scr = bufs[6]
    cls = [bufs[7 + 5 * g:12 + 5 * g] for g in range(n_pat)]
    res = bufs[7 + 5 * n_pat:]
    acc_n, l_n = res[:n_pat], res[n_pat:]
    dilated = [dil for _, dil in DIL_PATTERNS if dil > 1]
    merge_dil = min(dilated) if dilated else 1
    assert all(dil % merge_dil == 0 for dil in dilated)

    qi = lax.broadcasted_iota(jnp.int32, (2 * blk, blk), 0) % blk
    ki = lax.broadcasted_iota(jnp.int32, (2 * blk, blk), 1)
    bias_scr[:, 0:blk] = jnp.where(ki >= qi, 0.0, NEG_INF)
    bias_scr[:, blk:] = jnp.where(ki <= qi, 0.0, NEG_INF)
    lo_blk = lax.broadcasted_iota(jnp.int32, (blk, LANES), 1) < HEAD_DIM

    if any(dil > 1 for _, dil in DIL_PATTERNS):
        for dst, src in zip(natural, (q_ref, k_ref, v_ref)):
            dst[...] = src[...].astype(F32)

    staged_dil = 1
    for g, (window, dil) in enumerate(DIL_PATTERNS):
        assert window // dil == blk and seq_len % (blk * dil) == 0
        lc = seq_len // dil
        q0c, q1c, kc, vac, vbc = cls[g]
        refine = staged_dil > 1 and dil % staged_dil == 0
        stage_here = (dil > 1 and not refine
                      and any(d2 > dil and d2 % dil == 0 for _, d2 in DIL_PATTERNS[g + 1:]))
        for r in range(dil):
            base = r * lc
            if dil == 1:
                qv, kv, vv = q_ref[...], k_ref[...], v_ref[...]
            else:
                if refine:
                    first = (r % staged_dil) * (seq_len // staged_dil) + r // staged_dil
                    rows, srcs = pl.ds(first, lc, stride=dil // staged_dil), staged
                else:
                    rows, srcs = pl.ds(r, lc, stride=dil), natural
                qv, kv, vv = (s[rows, :] for s in srcs)
                if stage_here:
                    for dst, val in zip(staged, (qv, kv, vv)):
                        dst[base:base + lc, :] = val
            lo = lax.broadcasted_iota(jnp.int32, qv.shape, 1) < HEAD_DIM
            zero, one = jnp.zeros_like(qv), jnp.ones_like(vv)
            q0c[base:base + lc, :] = jnp.where(lo, qv, zero).astype(BF16)
            q1c[base:base + lc, :] = jnp.where(lo, zero, qv).astype(BF16)
            kc[base:base + lc, :] = kv.astype(BF16)
            vac[base:base + lc, :] = jnp.where(lo, vv, one).astype(BF16)
            vbc[base:base + lc, :] = jnp.where(lo, one, vv).astype(BF16)
            for n in range(lc // blk):
                cur = slice(base + n * blk, base + (n + 1) * blk)
                keys = slice(base + max(n - 1, 0) * blk, base + (n + 1) * blk)
                q2 = jnp.concatenate([q0c[cur, :], q1c[cur, :]], axis=0)
                bias = bias_scr[...] if n else bias_scr[:, blk:]
                e = jnp.exp2(_dot_nt(q2, kc[keys, :]) + bias).astype(BF16)
                ra = _dot(e[:blk], vac[keys, :])
                rb = _dot(e[blk:], vbc[keys, :])
                if dil == 1:
                    dst = pl.ds(n * blk, blk)
                else:
                    sub = dil // merge_dil
                    first = ((r % merge_dil) * (seq_len // merge_dil) + r // merge_dil
                             + n * blk * sub)
                    dst = pl.ds(first, blk, stride=sub) if sub > 1 else pl.ds(first, blk)
                acc_n[g][dst, :] = jnp.where(lo_blk, ra, rb)
                l_n[g][dst, :] = jnp.where(lo_blk, rb, ra)
        if stage_here:
            staged_dil = dil

    lcm = seq_len // merge_dil
    chunk = 256
    for r in range(merge_dil):
        for c in range(lcm // chunk):
            cm_rows = pl.ds(r * lcm + c * chunk, chunk)
            nat_rows = (pl.ds(r + c * chunk * merge_dil, chunk, stride=merge_dil)
                        if merge_dil > 1 else cm_rows)
            rows_of = lambda g: nat_rows if DIL_PATTERNS[g][1] == 1 else cm_rows
            num = functools.reduce(jnp.add, [acc_n[g][rows_of(g), :] for g in range(n_pat)])
            den = functools.reduce(jnp.add, [l_n[g][rows_of(g), :] for g in range(n_pat)])
            out_scr[nat_rows, :] = num / pltpu.roll(den, HEAD_DIM, 1)
    for c in range(seq_len // chunk):
        sl = slice(c * chunk, (c + 1) * chunk)
        o_ref[sl, :] = out_scr[sl, :].astype(BF16)


def _dilated_attention(q4, k4, v4, score_bound):
    n_pairs, bsz, seq_len, _ = q4.shape
    spec = pl.BlockSpec((None, None, seq_len, LANES), lambda b, h: (h, b, 0, 0))
    n_pat = len(DIL_PATTERNS)
    common = dict(
        grid=(bsz, n_pairs),
        in_specs=[spec, spec, spec],
        out_specs=spec,
        out_shape=jax.ShapeDtypeStruct(q4.shape, BF16),
        compiler_params=pltpu.CompilerParams(dimension_semantics=("arbitrary", "arbitrary"),
                                             vmem_limit_bytes=VMEM_LIMIT_BYTES))
    seq_f32 = lambda count: [pltpu.VMEM((seq_len, LANES), F32) for _ in range(count)]
    seq_bf16 = lambda count: [pltpu.VMEM((seq_len, LANES), BF16) for _ in range(count)]
    fast = pl.pallas_call(
        functools.partial(_dil_fast_kernel, seq_len=seq_len),
        scratch_shapes=[pltpu.VMEM((2 * DIL_BLOCK, 2 * DIL_BLOCK), F32)]
                       + seq_f32(7) + seq_bf16(5 * n_pat) + seq_f32(2 * n_pat),
        name="dilated_attn_fast", **common)
    safe = pl.pallas_call(
        functools.partial(_dil_attn_kernel, seq_len=seq_len),
        scratch_shapes=seq_f32(3) + seq_bf16(5) + seq_f32(3 * n_pat),
        name="dilated_attn", **common)
    return lax.cond(score_bound <= FAST_SCORE_BOUND, fast, safe, q4, k4, v4)


def _mix_ffn_tile(x, ys, wo_ref, g_ref, wg_ref, wu_ref, wd_ref, side_work=()):
    x1 = x
    off = 0
    for y in ys:
        width = y.shape[1]
        x1 = x1 + _dot(y, wo_ref[off:off + width, :])
        off += width
    h = _rms(x1, g_ref[...]).astype(BF16)
    acts = []
    n_chunks = wg_ref.shape[1] // MXU_DIM
    for c in range(n_chunks):
        cols = slice(c * MXU_DIM, (c + 1) * MXU_DIM)
        gate = _dot(h, wg_ref[:, cols])
        up = _dot(h, wu_ref[:, cols])
        acts.append((gate * jax.nn.sigmoid(gate) * up).astype(BF16))
        for work in side_work[c * len(side_work) // n_chunks:(c + 1) * len(side_work) // n_chunks]:
            work(gate[0:1, :])
    return x1 + _dot(jnp.concatenate(acts, axis=-1), wd_ref[...])


def _chunks_to_lanes(y_ref):
    return jnp.concatenate([y_ref[c] for c in range(y_ref.shape[0])], axis=-1)


def _mix_ffn_kernel(x_ref, y_ref, wo_ref, g_ref, wg_ref, wu_ref, wd_ref, o_ref):
    o_ref[...] = _mix_ffn_tile(x_ref[...], [_chunks_to_lanes(y_ref)], wo_ref, g_ref, wg_ref,
                               wu_ref, wd_ref)


RGLRU_PIECE_ROWS = 16


def _rglru_mix_ffn_kernel(x_ref, xr_ref, gate_ref, yd_ref, cw_ref, cb_ref, wax_ref, ba_ref, bx_ref,
                          lam_ref, wo_ref, g_ref, wg_ref, wu_ref, wd_ref, o_ref,
                          tail_scr, h_scr, y_scr, *, tiles_per_seq):
    j = pl.program_id(0)

    @pl.when(j == 0)
    def _():
        y_scr[...] = jnp.zeros_like(y_scr)

    @pl.when(j % tiles_per_seq == 0)
    def _():
        tail_scr[...] = jnp.zeros_like(tail_scr)
        h_scr[...] = jnp.zeros_like(h_scr)

    y_prev = y_scr[...]
    u, pre_r, pre_i = _rglru_gates(xr_ref[...], cw_ref, cb_ref, wax_ref, tail_scr)
    z = -lam_ref[...]
    softplus = jnp.maximum(z, 0.0) + jnp.log1p(jnp.exp(-jnp.abs(z)))
    ba, bx = ba_ref[...], bx_ref[...]
    state = [h_scr[0:1, :]]

    def piece(p):
        def run(anchor):
            floor = jnp.minimum(jnp.concatenate([anchor] * (RNN_WIDTH // anchor.shape[1]), axis=1),
                                -3.0e38)
            sl = slice(p * RGLRU_PIECE_ROWS, (p + 1) * RGLRU_PIECE_ROWS)
            y_rows, state[0] = _rglru_rows(u[sl], jnp.maximum(pre_r[sl], floor),
                                           jnp.maximum(pre_i[sl], floor), gate_ref[sl, :], ba, bx,
                                           softplus, state[0])
            y_scr[sl, :] = y_rows
        return run

    n_pieces = xr_ref.shape[0] // RGLRU_PIECE_ROWS
    o_ref[...] = _mix_ffn_tile(x_ref[...], [y_prev, _chunks_to_lanes(yd_ref)], wo_ref, g_ref,
                               wg_ref, wu_ref, wd_ref,
                               side_work=[piece(p) for p in range(n_pieces)])
    h_scr[...] = jnp.broadcast_to(state[0], h_scr.shape)


def _ffn_operands(w_out, norm_g, w_gate, w_up, w_down):
    d = norm_g.shape[0]
    operands = [w_out.astype(BF16), norm_g.reshape(1, d), w_gate.astype(BF16), w_up.astype(BF16),
                w_down.astype(BF16)]
    return operands, [_const_spec(op.shape) for op in operands]


def _mix_ffn(x2, y, w_out, norm_g, w_gate, w_up, w_down, *, tm):
    n, d = x2.shape
    row = lambda i: (i, 0)
    ffn_ops, ffn_specs = _ffn_operands(w_out, norm_g, w_gate, w_up, w_down)
    return pl.pallas_call(
        _mix_ffn_kernel,
        grid=(n // tm,),
        in_specs=[pl.BlockSpec((tm, d), row),
                  pl.BlockSpec((y.shape[0], tm, LANES), lambda i: (0, i, 0)), *ffn_specs],
        out_specs=pl.BlockSpec((tm, d), row),
        out_shape=jax.ShapeDtypeStruct((n, d), F32),
        compiler_params=pltpu.CompilerParams(dimension_semantics=("arbitrary",),
                                             vmem_limit_bytes=VMEM_LIMIT_BYTES),
        name="outproj_swiglu",
    )(x2, y, *ffn_ops)


def _rglru_mix_ffn(x2, xg, y_diff, seq_len, rglru_params, w_out, norm_g, w_gate, w_up, w_down,
                   *, tm):
    n, d = x2.shape
    w = RNN_WIDTH
    n_tiles = n // tm
    assert seq_len % tm == 0
    prev = lambda j: jnp.maximum(j - 1, 0)
    this = lambda j: jnp.minimum(j, n_tiles - 1)
    rg_ops, rg_specs = _rglru_operands(*rglru_params)
    ffn_ops, ffn_specs = _ffn_operands(w_out, norm_g, w_gate, w_up, w_down)
    return pl.pallas_call(
        functools.partial(_rglru_mix_ffn_kernel, tiles_per_seq=seq_len // tm),
        grid=(n_tiles + 1,),
        in_specs=[pl.BlockSpec((tm, d), lambda j: (prev(j), 0)),
                  pl.BlockSpec((tm, w), lambda j: (this(j), 0)),
                  pl.BlockSpec((tm, w), lambda j: (this(j), 1)),
                  pl.BlockSpec((y_diff.shape[0], tm, LANES), lambda j: (0, prev(j), 0)),
                  *rg_specs, *ffn_specs],
        out_specs=pl.BlockSpec((tm, d), lambda j: (prev(j), 0)),
        out_shape=jax.ShapeDtypeStruct((n, d), F32),
        scratch_shapes=[pltpu.VMEM((SUBLANES, w), F32), pltpu.VMEM((SUBLANES, w), F32),
                        pltpu.VMEM((tm, w), BF16)],
        compiler_params=pltpu.CompilerParams(dimension_semantics=("arbitrary",),
                                             vmem_limit_bytes=VMEM_LIMIT_BYTES),
        name="rglru_outproj_swiglu",
    )(x2, xg, xg, y_diff, *rg_ops, *ffn_ops)


def kernel(x, ab_norm_g, ab_w_in, ab_conv_w, ab_conv_b, ab_wa, ab_ba, ab_wx, ab_bx, ab_lru_lambda, ab_q_norm_g, ab_k_norm_g, ab_lambda_q1, ab_lambda_k1, ab_lambda_q2, ab_lambda_k2, ab_subln_g, ab_w_out, c_norm_g, c_w_qkv, c_q_norm_g, c_k_norm_g, c_w_out, ffn_norm_g, ffn_w_gate, ffn_w_up, ffn_w_down):
    bsz, seq_len, d = x.shape
    n = bsz * seq_len
    tables = _rope_tables(seq_len)
    x2 = x.reshape(n, d)

    xg, q, k, vt = _project(x2, seq_len, ab_norm_g[0], ab_w_in[0], ab_q_norm_g[0], ab_k_norm_g[0],
                            tables, plain_w=2 * RNN_WIDTH, qk_w=DIFF_QK_WIDTH, v_w=DIFF_WIDTH,
                            tm=512, v_transposed=True)
    to4 = lambda t: t.reshape(-1, bsz, seq_len, LANES)
    rows = lambda t: t.reshape(-1, n, LANES)
    score_bound = lambda gq, gk: (HEAD_DIM ** 0.5) * jnp.max(jnp.abs(gq)) * jnp.max(jnp.abs(gk))
    y_diff = _diff_attention(to4(q), to4(k), vt, ab_lambda_q1[0], ab_lambda_k1[0],
                             ab_lambda_q2[0], ab_lambda_k2[0], ab_subln_g[0],
                             score_bound(ab_q_norm_g[0], ab_k_norm_g[0]),
                             layer_idx=0, tq=256, tk=256)
    rglru_params = (ab_conv_w[0], ab_conv_b[0], ab_wa[0], ab_wx[0], ab_ba[0], ab_bx[0],
                    ab_lru_lambda[0])
    x2 = _rglru_mix_ffn(x2, xg, rows(y_diff), seq_len, rglru_params, ab_w_out[0], ffn_norm_g[0],
                        ffn_w_gate[0], ffn_w_up[0], ffn_w_down[0], tm=256)

    q, k, v = _project(x2, seq_len, c_norm_g[0], c_w_qkv[0], c_q_norm_g[0], c_k_norm_g[0],
                       tables, plain_w=0, qk_w=D_MODEL, v_w=D_MODEL, tm=512)
    o = _dilated_attention(to4(q), to4(k), to4(v), score_bound(c_q_norm_g[0], c_k_norm_g[0]))
    x2 = _mix_ffn(x2, rows(o), c_w_out[0], ffn_norm_g[1], ffn_w_gate[1], ffn_w_up[1],
                  ffn_w_down[1], tm=256)
    return x2.reshape(bsz, seq_len, d)
```

```python
import functools
import math

import jax
import jax.numpy as jnp
from jax import lax
from jax.experimental import pallas as pl
from jax.experimental.pallas import tpu as pltpu

D_MODEL = 1024
NORM_EPS = 1e-6
ROPE_THETA = 500000.0
HEAD_DIM = 64
ROT_DIM = HEAD_DIM // 4
ROT_HALF = ROT_DIM // 2
NEG_INF = -1e30
LOG2_E = math.log2(math.e)
RNN_WIDTH = D_MODEL // 2
RNN_BLOCKS = 8
RNN_BLOCK = RNN_WIDTH // RNN_BLOCKS
CONV_WIDTH = 4
RGLRU_C = 8.0
DIFF_HEADS = (D_MODEL // 2) // (2 * HEAD_DIM)
DIFF_QK_WIDTH = DIFF_HEADS * 2 * HEAD_DIM
DIFF_V_DIM = 2 * HEAD_DIM
DIFF_WIDTH = DIFF_HEADS * DIFF_V_DIM
DIL_HEADS = D_MODEL // HEAD_DIM
DIL_PATTERNS = ((128, 1), (512, 4), (2048, 16))
DIL_BLOCK = 128
D_FF = -(-8 * D_MODEL // (3 * 256)) * 256

LANES = 128
SUBLANES = 8
MXU_DIM = 256
VMEM_LIMIT_BYTES = 56 * 1024 * 1024

F32 = jnp.float32
BF16 = jnp.bfloat16


def _dot(a, b):
    return jnp.dot(a, b, preferred_element_type=F32)


def _dot_nt(a, b):
    return lax.dot_general(a, b, (((1,), (1,)), ((), ())), preferred_element_type=F32)


def _rms(xf, g):
    ms = jnp.mean(xf * xf, axis=-1, keepdims=True)
    return xf * lax.rsqrt(ms + NORM_EPS) * g


def _const_spec(shape):
    nd = len(shape)
    return pl.BlockSpec(shape, lambda *_: (0,) * nd, pipeline_mode=pl.Buffered(1))


def _head_group_norm_rope(t, head_mean, tab_c, tab_a, tab_b):
    inv = lax.rsqrt(_dot((t * t).astype(BF16), head_mean) + NORM_EPS)
    out = []
    for c in range(t.shape[1] // LANES):
        tc = t[:, c * LANES:(c + 1) * LANES]
        out.append(tc * tab_c + pltpu.roll(tc, ROT_HALF, 1) * tab_a
                   + pltpu.roll(tc, LANES - ROT_HALF, 1) * tab_b)
    return inv * jnp.concatenate(out, axis=1)


def _gained_tables(tables, gain, scale):
    cos_t, sin_a, sin_b = tables
    g2 = jnp.concatenate([gain, gain]).reshape(1, LANES).astype(F32)
    return (cos_t * g2 * scale, sin_a * jnp.roll(g2, ROT_HALF, axis=1) * scale,
            sin_b * jnp.roll(g2, -ROT_HALF, axis=1) * scale)


def _rope_tables(seq_len):
    pos = jnp.arange(seq_len, dtype=F32)
    inv_freq = 1.0 / (ROPE_THETA ** (jnp.arange(0, ROT_DIM, 2, dtype=F32) / ROT_DIM))
    ang = pos[:, None] * inv_freq[None, :]
    cos, sin = jnp.cos(ang), jnp.sin(ang)
    rest = HEAD_DIM - ROT_DIM
    one_r = jnp.ones((seq_len, rest), F32)
    zero_r = jnp.zeros((seq_len, rest), F32)
    zero_h = jnp.zeros((seq_len, ROT_HALF), F32)
    cos_t = jnp.concatenate([cos, cos, one_r], axis=-1)
    sin_a = jnp.concatenate([zero_h, sin, zero_r], axis=-1)
    sin_b = jnp.concatenate([-sin, zero_h, zero_r], axis=-1)
    pair = lambda t: jnp.concatenate([t, t], axis=-1)
    return pair(cos_t), pair(sin_a), pair(sin_b)


def _proj_kernel(x_ref, g_ref, w_ref, ones_ref, qc_ref, qa_ref, qb_ref, kc_ref, ka_ref, kb_ref,
                 *rest, plain_w, qk_w, v_w, v_transposed):
    h = _rms(x_ref[...], g_ref[...]).astype(BF16)
    if v_transposed:
        wvt_ref, *out_refs = rest
    else:
        out_refs = rest
    if plain_w:
        plain_ref, q_ref, k_ref, v_ref = out_refs
        plain_ref[...] = _dot(h, w_ref[:, 0:plain_w])
    else:
        q_ref, k_ref, v_ref = out_refs
    head_mean = ones_ref[...]
    per_group = MXU_DIM // LANES
    for ref, tabs, off in ((q_ref, (qc_ref, qa_ref, qb_ref), plain_w),
                           (k_ref, (kc_ref, ka_ref, kb_ref), plain_w + qk_w)):
        t_all = _dot(h, w_ref[:, off:off + qk_w])
        tab_c, tab_a, tab_b = (t[...] for t in tabs)
        for grp in range(qk_w // MXU_DIM):
            y = _head_group_norm_rope(t_all[:, grp * MXU_DIM:(grp + 1) * MXU_DIM], head_mean,
                                      tab_c, tab_a, tab_b).astype(BF16)
            for c in range(per_group):
                ref[grp * per_group + c] = y[:, c * LANES:(c + 1) * LANES]
    if v_transposed:
        v_ref[0] = _dot_nt(wvt_ref[...], h).astype(BF16)
    else:
        off = plain_w + 2 * qk_w
        v_all = _dot(h, w_ref[:, off:off + v_w]).astype(BF16)
        for c in range(v_w // LANES):
            v_ref[c] = v_all[:, c * LANES:(c + 1) * LANES]


def _project(x2, seq_len, norm_g, w, q_gain, k_gain, tables, *, plain_w, qk_w, v_w, tm,
             v_transposed=False):
    n, d = x2.shape
    n_out = plain_w + 2 * qk_w + v_w
    assert w.shape == (d, n_out) and n % tm == 0 and seq_len % tm == 0
    tiles_per_seq = seq_len // tm
    row = lambda i: (i, 0)
    pos = lambda i: (i % tiles_per_seq, 0)
    ones2 = jnp.kron(jnp.eye(MXU_DIM // HEAD_DIM, dtype=F32),
                     jnp.full((HEAD_DIM, HEAD_DIM), 1.0 / HEAD_DIM, F32)).astype(BF16)
    rope_in = (*_gained_tables(tables, q_gain, HEAD_DIM ** -0.5 * LOG2_E),
               *_gained_tables(tables, k_gain, 1.0))
    out_shape, out_specs = [], []
    if plain_w:
        out_shape.append(jax.ShapeDtypeStruct((n, plain_w), F32))
        out_specs.append(pl.BlockSpec((tm, plain_w), row))
    chunked = lambda width: (jax.ShapeDtypeStruct((width // LANES, n, LANES), BF16),
                             pl.BlockSpec((width // LANES, tm, LANES), lambda i: (0, i, 0)))
    for width in (qk_w, qk_w):
        shape, spec = chunked(width)
        out_shape.append(shape)
        out_specs.append(spec)
    operands = [x2, norm_g.reshape(1, d), w.astype(BF16), ones2, *rope_in]
    in_specs = [pl.BlockSpec((tm, d), row), _const_spec((1, d)), _const_spec((d, n_out)),
                _const_spec(ones2.shape)] + [pl.BlockSpec((tm, LANES), pos) for _ in rope_in]
    if v_transposed:
        operands.append(w[:, n_out - v_w:].T.astype(BF16))
        in_specs.append(_const_spec((v_w, d)))
        out_shape.append(jax.ShapeDtypeStruct((n // seq_len, v_w, seq_len), BF16))
        out_specs.append(pl.BlockSpec((1, v_w, tm),
                                      lambda i: (i // tiles_per_seq, 0, i % tiles_per_seq)))
    else:
        shape, spec = chunked(v_w)
        out_shape.append(shape)
        out_specs.append(spec)
    return pl.pallas_call(
        functools.partial(_proj_kernel, plain_w=plain_w, qk_w=qk_w, v_w=v_w,
                          v_transposed=v_transposed),
        grid=(n // tm,),
        in_specs=in_specs,
        out_specs=out_specs,
        out_shape=out_shape,
        compiler_params=pltpu.CompilerParams(dimension_semantics=("arbitrary",),
                                             vmem_limit_bytes=VMEM_LIMIT_BYTES),
        name="proj_qk_rope",
    )(*operands)


def _rglru_gates(x, cw_ref, cb_ref, wax_ref, tail_scr):
    ts = x.shape[0]
    xe = jnp.concatenate([tail_scr[...], x], axis=0)
    cw = cw_ref[...]
    u = cb_ref[...] + x * cw[CONV_WIDTH - 1:CONV_WIDTH]
    for back in range(1, CONV_WIDTH):
        shifted = pltpu.roll(xe, back, 0)[SUBLANES:]
        u = u + shifted * cw[CONV_WIDTH - 1 - back:CONV_WIDTH - back]
    tail_scr[...] = x[ts - SUBLANES:]

    ub = u.astype(BF16)
    half = RNN_WIDTH // 2
    ra0 = _dot(ub[:, :half], wax_ref[0])
    ra1 = _dot(ub[:, half:], wax_ref[1])
    pre_r = jnp.concatenate([ra0[:, :half], ra1[:, :half]], axis=-1)
    pre_i = jnp.concatenate([ra0[:, half:], ra1[:, half:]], axis=-1)
    return u, pre_r, pre_i


def _rglru_rows(u, pre_r, pre_i, gate, ba, bx, softplus, carry):
    rows = u.shape[0]
    r = jax.nn.sigmoid(pre_r + ba)
    i = jax.nn.sigmoid(pre_i + bx)
    log_a = (-RGLRU_C) * r * softplus
    a = jnp.exp(log_a)
    b = jnp.sqrt(-jnp.tanh(log_a) * (a * a + 1.0)) * (i * u)

    grouped = (rows // SUBLANES, SUBLANES, a.shape[1])
    a, b = a.reshape(grouped), b.reshape(grouped)
    row_in_group = lax.broadcasted_iota(jnp.int32, grouped, 1)
    step = 1
    while step < SUBLANES:
        keep = row_in_group >= step
        a_prev = jnp.where(keep, pltpu.roll(a, step, 1), 1.0)
        b_prev = jnp.where(keep, pltpu.roll(b, step, 1), 0.0)
        b = a * b_prev + b
        a = a * a_prev
        step *= 2
    a, b = a.reshape(rows, grouped[2]), b.reshape(rows, grouped[2])
    groups = []
    for grp in range(rows // SUBLANES):
        sl = slice(grp * SUBLANES, (grp + 1) * SUBLANES)
        h_grp = b[sl] + a[sl] * carry
        groups.append(h_grp)
        carry = h_grp[SUBLANES - 1:SUBLANES, :]
    h = jnp.concatenate(groups, axis=0)
    return (h * jax.nn.gelu(gate)).astype(BF16), carry


def _rglru_operands(conv_w, conv_b, wa, wx, ba, bx, lru_lambda):
    w = RNN_WIDTH
    half = w // 2
    per_half = RNN_BLOCKS // 2

    def block_diag(wt):
        out = jnp.zeros((2, half, half), F32)
        for g in range(RNN_BLOCKS):
            j = (g % per_half) * RNN_BLOCK
            out = out.at[g // per_half, j:j + RNN_BLOCK, j:j + RNN_BLOCK].set(wt[g])
        return out

    wax = jnp.concatenate([block_diag(wa), block_diag(wx)], axis=-1).astype(BF16)
    vec = lambda v: v.reshape(1, w).astype(F32)
    operands = [conv_w.astype(F32), vec(conv_b), wax, vec(ba), vec(bx), vec(lru_lambda)]
    return operands, [_const_spec(op.shape) for op in operands]


def _diff_attn_kernel(q_ref, k_ref, v_ref, lq1_ref, lk1_ref, lq2_ref, lk2_ref, sg_ref, o_ref,
                      *, tq, tk, lambda_init):
    qi = pl.program_id(2)
    q = q_ref[...]
    lo = lax.broadcasted_iota(jnp.int32, q.shape, 1) < HEAD_DIM
    zero = jnp.zeros_like(q)
    q_parts = (jnp.where(lo, q, zero), jnp.where(lo, zero, q))
    q_pos = qi * tq + lax.broadcasted_iota(jnp.int32, (tq, tk), 0)
    k_off = lax.broadcasted_iota(jnp.int32, (tq, tk), 1)

    def body(j, carry):
        start = pl.multiple_of(j * tk, tk)
        kb = k_ref[pl.ds(start, tk), :]
        vb = v_ref[pl.ds(start, tk), :]
        visible = (k_off + start) <= q_pos
        new = []
        for c in range(2):
            m, l, acc = carry[c]
            s = jnp.where(visible, _dot_nt(q_parts[c], kb), NEG_INF)
            m_new = jnp.maximum(m, jnp.max(s, axis=-1, keepdims=True))
            alpha = jnp.exp2(m - m_new)
            e = jnp.exp2(s - m_new)
            l = alpha * l + jnp.sum(e, axis=-1, keepdims=True)
            acc = alpha * acc + _dot(e.astype(BF16), vb)
            new.append((m_new, l, acc))
        return tuple(new)

    init = tuple((jnp.full((tq, 1), NEG_INF, F32), jnp.zeros((tq, 1), F32),
                  jnp.zeros((tq, DIFF_V_DIM), F32)) for _ in range(2))
    n_kv = (qi * tq + tq + tk - 1) // tk
    (_, l0, acc0), (_, l1, acc1) = lax.fori_loop(0, n_kv, body, init)

    dot_l = lambda a, b: jnp.sum(a[...] * b[...], axis=-1, keepdims=True)
    lam = jnp.exp(dot_l(lq1_ref, lk1_ref)) - jnp.exp(dot_l(lq2_ref, lk2_ref)) + lambda_init
    o = acc0 / l0 - lam * (acc1 / l1)
    o_ref[...] = (_rms(o, sg_ref[...]) * (1.0 - lambda_init)).astype(BF16)


def _diff_fast_kernel(q_ref, k_ref, vt_ref, lq1_ref, lk1_ref, lq2_ref, lk2_ref, sg_ref, o_ref,
                      s_scr, *, seq_len, tq, lambda_init):
    dot_l = lambda a, b: jnp.sum(a[...] * b[...], axis=-1, keepdims=True)
    lam = jnp.exp(dot_l(lq1_ref, lk1_ref)) - jnp.exp(dot_l(lq2_ref, lk2_ref)) + lambda_init
    lo = lax.broadcasted_iota(jnp.int32, (tq, LANES), 1) < HEAD_DIM
    key_row = lax.broadcasted_iota(jnp.int32, (tq, 2 * tq), 0)
    q_col = lax.broadcasted_iota(jnp.int32, (tq, 2 * tq), 1) % tq
    causal = key_row <= q_col
    for i in range(seq_len // tq):
        qt = q_ref[i * tq:(i + 1) * tq, :]
        zero = jnp.zeros_like(qt)
        q2 = jnp.concatenate([jnp.where(lo, qt, zero), jnp.where(lo, zero, qt)], axis=0)
        n_keys = (i + 1) * tq
        s_scr[0:n_keys, :] = _dot_nt(k_ref[0:n_keys, :], q2)
        acc = jnp.zeros((DIFF_V_DIM, 2 * tq), F32)
        l8 = jnp.zeros((SUBLANES, 2 * tq), F32)
        for j in range(i + 1):
            s = s_scr[j * tq:(j + 1) * tq, :]
            if j == i:
                s = jnp.where(causal, s, NEG_INF)
            e = jnp.exp2(s)
            l8 = l8 + jnp.sum(e.reshape(tq // SUBLANES, SUBLANES, 2 * tq), axis=0)
            acc = acc + _dot(vt_ref[0, :, j * tq:(j + 1) * tq], e.astype(BF16))
        l = jnp.sum(l8, axis=0, keepdims=True)
        o = acc[:, :tq] / l[:, :tq] - lam * (acc[:, tq:] / l[:, tq:])
        ms = jnp.mean(o * o, axis=0, keepdims=True)
        y = o * lax.rsqrt(ms + NORM_EPS) * sg_ref[...] * (1.0 - lambda_init)
        o_ref[i * tq:(i + 1) * tq, :] = y.T.astype(BF16)


FAST_SCORE_BOUND = 60.0


def _diff_attention(q4, k4, vt3, lq1, lk1, lq2, lk2, subln_g, score_bound, *, layer_idx, tq, tk):
    _, bsz, seq_len, _ = q4.shape
    lambda_init = 0.8 - 0.6 * math.exp(-0.3 * layer_idx)
    vec = lambda v: v.reshape(1, -1).astype(F32)
    lam_ops = (vec(lq1), vec(lk1), vec(lq2), vec(lk2))
    lam_specs = [_const_spec((1, HEAD_DIM)) for _ in range(4)]
    seq_spec = pl.BlockSpec((None, None, seq_len, LANES), lambda b, h: (h, b, 0, 0))
    out_shape = jax.ShapeDtypeStruct((DIFF_HEADS, bsz, seq_len, LANES), BF16)

    def fast(q3, k3, vt3):
        gain_cols = jnp.broadcast_to(subln_g.astype(F32)[:, None], (DIFF_V_DIM, tq))
        return pl.pallas_call(
            functools.partial(_diff_fast_kernel, seq_len=seq_len, tq=tq, lambda_init=lambda_init),
            grid=(bsz, DIFF_HEADS),
            in_specs=[seq_spec, seq_spec, pl.BlockSpec((1, LANES, seq_len), lambda b, h: (b, h, 0)),
                      *lam_specs, _const_spec((DIFF_V_DIM, tq))],
            out_specs=seq_spec,
            out_shape=out_shape,
            scratch_shapes=[pltpu.VMEM((seq_len, 2 * tq), F32)],
            compiler_params=pltpu.CompilerParams(dimension_semantics=("arbitrary", "arbitrary"),
                                                 vmem_limit_bytes=VMEM_LIMIT_BYTES),
            name="diff_attn_fast",
        )(q3, k3, vt3, *lam_ops, gain_cols)

    def safe(q3, k3, vt3):
        kv_spec = pl.BlockSpec((None, None, seq_len, LANES), lambda b, h, i: (h, b, 0, 0))
        q_spec = pl.BlockSpec((None, None, tq, LANES), lambda b, h, i: (h, b, i, 0))
        v4 = vt3.reshape(bsz, DIFF_HEADS, DIFF_V_DIM, seq_len).transpose(1, 0, 3, 2)
        return pl.pallas_call(
            functools.partial(_diff_attn_kernel, tq=tq, tk=tk, lambda_init=lambda_init),
            grid=(bsz, DIFF_HEADS, seq_len // tq),
            in_specs=[q_spec, kv_spec, kv_spec, *lam_specs, _const_spec((1, DIFF_V_DIM))],
            out_specs=q_spec,
            out_shape=out_shape,
            compiler_params=pltpu.CompilerParams(
                dimension_semantics=("arbitrary", "arbitrary", "arbitrary"),
                vmem_limit_bytes=VMEM_LIMIT_BYTES),
            name="diff_attn",
        )(q3, k3, v4, *lam_ops, vec(subln_g))

    return lax.cond(score_bound <= FAST_SCORE_BOUND, fast, safe, q4, k4, vt3)


def _dil_unit(q0, q1, k2, va, vb, band):
    lo = lax.broadcasted_iota(jnp.int32, (DIL_BLOCK, LANES), 1) < HEAD_DIM
    res, maxes = [], []
    for qh, vh in ((q0, va), (q1, vb)):
        s = jnp.where(band, _dot_nt(qh, k2), NEG_INF)
        m = s[:, :LANES]
        for c in range(1, s.shape[1] // LANES):
            m = jnp.maximum(m, s[:, c * LANES:(c + 1) * LANES])
        m = jnp.max(m, axis=-1, keepdims=True)
        e = jnp.exp2(s - m).astype(BF16)
        res.append(_dot(e, vh))
        maxes.append(m)
    acc = jnp.where(lo, res[0], res[1])
    l = pltpu.roll(jnp.where(lo, res[1], res[0]), HEAD_DIM, 1)
    return acc, jnp.where(lo, maxes[0], maxes[1]), l


def _dil_attn_kernel(q_ref, k_ref, v_ref, o_ref, qf, kf, vf, q0c, q1c, kc, vac, vbc,
                     *nat, seq_len):
    blk = DIL_BLOCK
    n_pat = len(DIL_PATTERNS)
    acc_n, m_n, l_n = nat[:n_pat], nat[n_pat:2 * n_pat], nat[2 * n_pat:]
    qf[...] = q_ref[...].astype(F32)
    kf[...] = k_ref[...].astype(F32)
    vf[...] = v_ref[...].astype(F32)

    qi = lax.broadcasted_iota(jnp.int32, (blk, blk), 0)
    ki = lax.broadcasted_iota(jnp.int32, (blk, blk), 1)
    band_cur = ki <= qi
    band_two = jnp.concatenate([ki >= qi, band_cur], axis=1)

    for g, (window, dil) in enumerate(DIL_PATTERNS):
        assert window // dil == blk and seq_len % (blk * dil) == 0
        lc = seq_len // dil
        for r in range(dil):
            rows = pl.ds(r, lc, stride=dil) if dil > 1 else pl.ds(0, lc)
            qv, vv = qf[rows, :], vf[rows, :]
            lo = lax.broadcasted_iota(jnp.int32, qv.shape, 1) < HEAD_DIM
            q0c[0:lc, :] = jnp.where(lo, qv, 0.0).astype(BF16)
            q1c[0:lc, :] = jnp.where(lo, 0.0, qv).astype(BF16)
            kc[0:lc, :] = kf[rows, :].astype(BF16)
            vac[0:lc, :] = jnp.where(lo, vv, 1.0).astype(BF16)
            vbc[0:lc, :] = jnp.where(lo, 1.0, vv).astype(BF16)
            for n in range(lc // blk):
                cur = slice(n * blk, (n + 1) * blk)
                keys = slice(max(n - 1, 0) * blk, (n + 1) * blk)
                acc, m, l = _dil_unit(q0c[cur, :], q1c[cur, :], kc[keys, :], vac[keys, :],
                                      vbc[keys, :], band_two if n else band_cur)
                first = n * blk * dil + r
                dst = pl.ds(first, blk, stride=dil) if dil > 1 else pl.ds(first, blk)
                acc_n[g][dst, :] = acc
                m_n[g][dst, :] = m
                l_n[g][dst, :] = l

    chunk = 256
    for c in range(seq_len // chunk):
        sl = slice(c * chunk, (c + 1) * chunk)
        ms = [m_n[g][sl, :] for g in range(n_pat)]
        m_all = functools.reduce(jnp.maximum, ms)
        num = jnp.zeros((chunk, LANES), F32)
        den = jnp.zeros((chunk, LANES), F32)
        for g in range(n_pat):
            wgt = jnp.exp2(ms[g] - m_all)
            num = num + wgt * acc_n[g][sl, :]
            den = den + wgt * l_n[g][sl, :]
        o_ref[sl, :] = (num / den).astype(BF16)


def _dil_fast_kernel(q_ref, k_ref, v_ref, o_ref, bias_scr, *bufs, seq_len):
    blk = DIL_BLOCK
    n_pat = len(DIL_PATTERNS)
    natural, staged = bufs[0:3], bufs[3:6]
    out_scr = bufs[6]
    cls = [bufs[7 + 5 * g:12 + 5 * g] for g in range(n_pat)]
    res = bufs[7 + 5 * n_pat:]
    acc_n, l_n = res[:n_pat], res[n_pat:]
    dilated = [dil for _, dil in DIL_PATTERNS if dil > 1]
    merge_dil = min(dilated) if dilated else 1
    assert all(dil % merge_dil == 0 for dil in dilated)

    qi = lax.broadcasted_iota(jnp.int32, (2 * blk, blk), 0) % blk
    ki = lax.broadcasted_iota(jnp.int32, (2 * blk, blk), 1)
    bias_scr[:, 0:blk] = jnp.where(ki >= qi, 0.0, NEG_INF)
    bias_scr[:, blk:] = jnp.where(ki <= qi, 0.0, NEG_INF)
    lo_blk = lax.broadcasted_iota(jnp.int32, (blk, LANES), 1) < HEAD_DIM

    if any(dil > 1 for _, dil in DIL_PATTERNS):
        for dst, src in zip(natural, (q_ref, k_ref, v_ref)):
            dst[...] = src[...].astype(F32)

    staged_dil = 1
    for g, (window, dil) in enumerate(DIL_PATTERNS):
        assert window // dil == blk and seq_len % (blk * dil) == 0
        lc = seq_len // dil
        q0c, q1c, kc, vac, vbc = cls[g]
        refine = staged_dil > 1 and dil % staged_dil == 0
        stage_here = (dil > 1 and not refine
                      and any(d2 > dil and d2 % dil == 0 for _, d2 in DIL_PATTERNS[g + 1:]))
        for r in range(dil):
            base = r * lc
            if dil == 1:
                qv, kv, vv = q_ref[...], k_ref[...], v_ref[...]
            else:
                if refine:
                    first = (r % staged_dil) * (seq_len // staged_dil) + r // staged_dil
                    rows, srcs = pl.ds(first, lc, stride=dil // staged_dil), staged
                else:
                    rows, srcs = pl.ds(r, lc, stride=dil), natural
                qv, kv, vv = (s[rows, :] for s in srcs)
                if stage_here:
                    for dst, val in zip(staged, (qv, kv, vv)):
                        dst[base:base + lc, :] = val
            lo = lax.broadcasted_iota(jnp.int32, qv.shape, 1) < HEAD_DIM
            zero, one = jnp.zeros_like(qv), jnp.ones_like(vv)
            q0c[base:base + lc, :] = jnp.where(lo, qv, zero).astype(BF16)
            q1c[base:base + lc, :] = jnp.where(lo, zero, qv).astype(BF16)
            kc[base:base + lc, :] = kv.astype(BF16)
            vac[base:base + lc, :] = jnp.where(lo, vv, one).astype(BF16)
            vbc[base:base + lc, :] = jnp.where(lo, one, vv).astype(BF16)
            for n in range(lc // blk):
                cur = slice(base + n * blk, base + (n + 1) * blk)
                keys = slice(base + max(n - 1, 0) * blk, base + (n + 1) * blk)
                q2 = jnp.concatenate([q0c[cur, :], q1c[cur, :]], axis=0)
                bias = bias_scr[...] if n else bias_scr[:, blk:]
                e = jnp.exp2(_dot_nt(q2, kc[keys, :]) + bias).astype(BF16)
                ra = _dot(e[:blk], vac[keys, :])
                rb = _dot(e[blk:], vbc[keys, :])
                if dil == 1:
                    dst = pl.ds(n * blk, blk)
                else:
                    sub = dil // merge_dil
                    first = ((r % merge_dil) * (seq_len // merge_dil) + r // merge_dil
                             + n * blk * sub)
                    dst = pl.ds(first, blk, stride=sub) if sub > 1 else pl.ds(first, blk)
                acc_n[g][dst, :] = jnp.where(lo_blk, ra, rb)
                l_n[g][dst, :] = jnp.where(lo_blk, rb, ra)
        if stage_here:
            staged_dil = dil

    lcm = seq_len // merge_dil
    chunk = 256
    for r in range(merge_dil):
        for c in range(lcm // chunk):
            cm_rows = pl.ds(r * lcm + c * chunk, chunk)
            nat_rows = (pl.ds(r + c * chunk * merge_dil, chunk, stride=merge_dil)
                        if merge_dil > 1 else cm_rows)
            rows_of = lambda g: nat_rows if DIL_PATTERNS[g][1] == 1 else cm_rows
            num = functools.reduce(jnp.add, [acc_n[g][rows_of(g), :] for g in range(n_pat)])
            den = functools.reduce(jnp.add, [l_n[g][rows_of(g), :] for g in range(n_pat)])
            out_scr[nat_rows, :] = num / pltpu.roll(den, HEAD_DIM, 1)
    for c in range(seq_len // chunk):
        sl = slice(c * chunk, (c + 1) * chunk)
        o_ref[sl, :] = out_scr[sl, :].astype(BF16)


def _dilated_attention(q4, k4, v4, score_bound):
    n_pairs, bsz, seq_len, _ = q4.shape
    spec = pl.BlockSpec((None, None, seq_len, LANES), lambda b, h: (h, b, 0, 0))
    n_pat = len(DIL_PATTERNS)
    common = dict(
        grid=(bsz, n_pairs),
        in_specs=[spec, spec, spec],
        out_specs=spec,
        out_shape=jax.ShapeDtypeStruct(q4.shape, BF16),
        compiler_params=pltpu.CompilerParams(dimension_semantics=("arbitrary", "arbitrary"),
                                             vmem_limit_bytes=VMEM_LIMIT_BYTES))
    seq_f32 = lambda count: [pltpu.VMEM((seq_len, LANES), F32) for _ in range(count)]
    seq_bf16 = lambda count: [pltpu.VMEM((seq_len, LANES), BF16) for _ in range(count)]
    fast = pl.pallas_call(
        functools.partial(_dil_fast_kernel, seq_len=seq_len),
        scratch_shapes=[pltpu.VMEM((2 * DIL_BLOCK, 2 * DIL_BLOCK), F32)]
                       + seq_f32(7) + seq_bf16(5 * n_pat) + seq_f32(2 * n_pat),
        name="dilated_attn_fast", **common)
    safe = pl.pallas_call(
        functools.partial(_dil_attn_kernel, seq_len=seq_len),
        scratch_shapes=seq_f32(3) + seq_bf16(5) + seq_f32(3 * n_pat),
        name="dilated_attn", **common)
    return lax.cond(score_bound <= FAST_SCORE_BOUND, fast, safe, q4, k4, v4)


FFN_SUB_ROWS = 256


def _mix_ffn_tile(x, ys, wo_ref, g_ref, wg_ref, wu_ref, wd_ref, side_work=(), n_split=1):
    rows = x.shape[0] // n_split
    subs = [slice(s * rows, (s + 1) * rows) for s in range(n_split)]
    x1s, hs = [], []
    for sub in subs:
        x1 = x[sub]
        off = 0
        for y in ys:
            width = y.shape[1]
            x1 = x1 + _dot(y[sub], wo_ref[off:off + width, :])
            off += width
        x1s.append(x1)
        hs.append(_rms(x1, g_ref[...]).astype(BF16))
    n_chunks = wg_ref.shape[1] // MXU_DIM
    slots = n_chunks * n_split
    acts = [[] for _ in subs]
    slot = 0
    for c in range(n_chunks):
        cols = slice(c * MXU_DIM, (c + 1) * MXU_DIM)
        for s in range(n_split):
            gate = _dot(hs[s], wg_ref[:, cols])
            up = _dot(hs[s], wu_ref[:, cols])
            acts[s].append((gate * jax.nn.sigmoid(gate) * up).astype(BF16))
            for work in side_work[slot * len(side_work) // slots:
                                  (slot + 1) * len(side_work) // slots]:
                work(gate[0:1, :])
            slot += 1
    outs = [x1s[s] + _dot(jnp.concatenate(acts[s], axis=-1), wd_ref[...]) for s in range(n_split)]
    return outs[0] if n_split == 1 else jnp.concatenate(outs, axis=0)


def _chunks_to_lanes(y_ref):
    return jnp.concatenate([y_ref[c] for c in range(y_ref.shape[0])], axis=-1)


def _mix_ffn_kernel(x_ref, y_ref, wo_ref, g_ref, wg_ref, wu_ref, wd_ref, o_ref):
    o_ref[...] = _mix_ffn_tile(x_ref[...], [_chunks_to_lanes(y_ref)], wo_ref, g_ref, wg_ref,
                               wu_ref, wd_ref, n_split=x_ref.shape[0] // FFN_SUB_ROWS)


RGLRU_PIECE_ROWS = 16


def _rglru_mix_ffn_kernel(x_ref, xr_ref, gate_ref, yd_ref, cw_ref, cb_ref, wax_ref, ba_ref, bx_ref,
                          lam_ref, wo_ref, g_ref, wg_ref, wu_ref, wd_ref, o_ref,
                          tail_scr, h_scr, y_scr, *, tiles_per_seq):
    j = pl.program_id(0)

    @pl.when(j == 0)
    def _():
        y_scr[...] = jnp.zeros_like(y_scr)

    @pl.when(j % tiles_per_seq == 0)
    def _():
        tail_scr[...] = jnp.zeros_like(tail_scr)
        h_scr[...] = jnp.zeros_like(h_scr)

    y_prev = y_scr[...]
    u, pre_r, pre_i = _rglru_gates(xr_ref[...], cw_ref, cb_ref, wax_ref, tail_scr)
    z = -lam_ref[...]
    softplus = jnp.maximum(z, 0.0) + jnp.log1p(jnp.exp(-jnp.abs(z)))
    ba, bx = ba_ref[...], bx_ref[...]
    state = [h_scr[0:1, :]]

    def piece(p):
        def run(anchor):
            floor = jnp.minimum(jnp.concatenate([anchor] * (RNN_WIDTH // anchor.shape[1]), axis=1),
                                -3.0e38)
            sl = slice(p * RGLRU_PIECE_ROWS, (p + 1) * RGLRU_PIECE_ROWS)
            y_rows, state[0] = _rglru_rows(u[sl], jnp.maximum(pre_r[sl], floor),
                                           jnp.maximum(pre_i[sl], floor), gate_ref[sl, :], ba, bx,
                                           softplus, state[0])
            y_scr[sl, :] = y_rows
        return run

    n_pieces = xr_ref.shape[0] // RGLRU_PIECE_ROWS
    o_ref[...] = _mix_ffn_tile(x_ref[...], [y_prev, _chunks_to_lanes(yd_ref)], wo_ref, g_ref,
                               wg_ref, wu_ref, wd_ref,
                               side_work=[piece(p) for p in range(n_pieces)],
                               n_split=x_ref.shape[0] // FFN_SUB_ROWS)
    h_scr[...] = jnp.broadcast_to(state[0], h_scr.shape)


def _ffn_operands(w_out, norm_g, w_gate, w_up, w_down):
    d = norm_g.shape[0]
    operands = [w_out.astype(BF16), norm_g.reshape(1, d), w_gate.astype(BF16), w_up.astype(BF16),
                w_down.astype(BF16)]
    return operands, [_const_spec(op.shape) for op in operands]


def _mix_ffn(x2, y, w_out, norm_g, w_gate, w_up, w_down, *, tm):
    n, d = x2.shape
    row = lambda i: (i, 0)
    ffn_ops, ffn_specs = _ffn_operands(w_out, norm_g, w_gate, w_up, w_down)
    return pl.pallas_call(
        _mix_ffn_kernel,
        grid=(n // tm,),
        in_specs=[pl.BlockSpec((tm, d), row),
                  pl.BlockSpec((y.shape[0], tm, LANES), lambda i: (0, i, 0)), *ffn_specs],
        out_specs=pl.BlockSpec((tm, d), row),
        out_shape=jax.ShapeDtypeStruct((n, d), F32),
        compiler_params=pltpu.CompilerParams(dimension_semantics=("arbitrary",),
                                             vmem_limit_bytes=VMEM_LIMIT_BYTES),
        name="outproj_swiglu",
    )(x2, y, *ffn_ops)


def _rglru_mix_ffn(x2, xg, y_diff, seq_len, rglru_params, w_out, norm_g, w_gate, w_up, w_down,
                   *, tm):
    n, d = x2.shape
    w = RNN_WIDTH
    n_tiles = n // tm
    assert seq_len % tm == 0
    prev = lambda j: jnp.maximum(j - 1, 0)
    this = lambda j: jnp.minimum(j, n_tiles - 1)
    rg_ops, rg_specs = _rglru_operands(*rglru_params)
    ffn_ops, ffn_specs = _ffn_operands(w_out, norm_g, w_gate, w_up, w_down)
    return pl.pallas_call(
        functools.partial(_rglru_mix_ffn_kernel, tiles_per_seq=seq_len // tm),
        grid=(n_tiles + 1,),
        in_specs=[pl.BlockSpec((tm, d), lambda j: (prev(j), 0)),
                  pl.BlockSpec((tm, w), lambda j: (this(j), 0)),
                  pl.BlockSpec((tm, w), lambda j: (this(j), 1)),
                  pl.BlockSpec((y_diff.shape[0], tm, LANES), lambda j: (0, prev(j), 0)),
                  *rg_specs, *ffn_specs],
        out_specs=pl.BlockSpec((tm, d), lambda j: (prev(j), 0)),
        out_shape=jax.ShapeDtypeStruct((n, d), F32),
        scratch_shapes=[pltpu.VMEM((SUBLANES, w), F32), pltpu.VMEM((SUBLANES, w), F32),
                        pltpu.VMEM((tm, w), BF16)],
        compiler_params=pltpu.CompilerParams(dimension_semantics=("arbitrary",),
                                             vmem_limit_bytes=VMEM_LIMIT_BYTES),
        name="rglru_outproj_swiglu",
    )(x2, xg, xg, y_diff, *rg_ops, *ffn_ops)


def kernel(x, ab_norm_g, ab_w_in, ab_conv_w, ab_conv_b, ab_wa, ab_ba, ab_wx, ab_bx, ab_lru_lambda, ab_q_norm_g, ab_k_norm_g, ab_lambda_q1, ab_lambda_k1, ab_lambda_q2, ab_lambda_k2, ab_subln_g, ab_w_out, c_norm_g, c_w_qkv, c_q_norm_g, c_k_norm_g, c_w_out, ffn_norm_g, ffn_w_gate, ffn_w_up, ffn_w_down):
    bsz, seq_len, d = x.shape
    n = bsz * seq_len
    tables = _rope_tables(seq_len)
    x2 = x.reshape(n, d)

    xg, q, k, vt = _project(x2, seq_len, ab_norm_g[0], ab_w_in[0], ab_q_norm_g[0], ab_k_norm_g[0],
                            tables, plain_w=2 * RNN_WIDTH, qk_w=DIFF_QK_WIDTH, v_w=DIFF_WIDTH,
                            tm=512, v_transposed=True)
    to4 = lambda t: t.reshape(-1, bsz, seq_len, LANES)
    rows = lambda t: t.reshape(-1, n, LANES)
    score_bound = lambda gq, gk: (HEAD_DIM ** 0.5) * jnp.max(jnp.abs(gq)) * jnp.max(jnp.abs(gk))
    y_diff = _diff_attention(to4(q), to4(k), vt, ab_lambda_q1[0], ab_lambda_k1[0],
                             ab_lambda_q2[0], ab_lambda_k2[0], ab_subln_g[0],
                             score_bound(ab_q_norm_g[0], ab_k_norm_g[0]),
                             layer_idx=0, tq=256, tk=256)
    rglru_params = (ab_conv_w[0], ab_conv_b[0], ab_wa[0], ab_wx[0], ab_ba[0], ab_bx[0],
                    ab_lru_lambda[0])
    x2 = _rglru_mix_ffn(x2, xg, rows(y_diff), seq_len, rglru_params, ab_w_out[0], ffn_norm_g[0],
                        ffn_w_gate[0], ffn_w_up[0], ffn_w_down[0], tm=512)

    q, k, v = _project(x2, seq_len, c_norm_g[0], c_w_qkv[0], c_q_norm_g[0], c_k_norm_g[0],
                       tables, plain_w=0, qk_w=D_MODEL, v_w=D_MODEL, tm=512)
    o = _dilated_attention(to4(q), to4(k), to4(v), score_bound(c_q_norm_g[0], c_k_norm_g[0]))
    x2 = _mix_ffn(x2, rows(o), c_w_out[0], ffn_norm_g[1], ffn_w_gate[1], ffn_w_up[1],
                  ffn_w_down[1], tm=512)
    return x2.reshape(bsz, seq_len, d)
```

```python
import functools
import math

import jax
import jax.numpy as jnp
from jax import lax
from jax.experimental import pallas as pl
from jax.experimental.pallas import tpu as pltpu

D_MODEL = 1024
NORM_EPS = 1e-6
ROPE_THETA = 500000.0
HEAD_DIM = 64
ROT_DIM = HEAD_DIM // 4
ROT_HALF = ROT_DIM // 2
NEG_INF = -1e30
LOG2_E = math.log2(math.e)
RNN_WIDTH = D_MODEL // 2
RNN_BLOCKS = 8
RNN_BLOCK = RNN_WIDTH // RNN_BLOCKS
CONV_WIDTH = 4
RGLRU_C = 8.0
DIFF_HEADS = (D_MODEL // 2) // (2 * HEAD_DIM)
DIFF_QK_WIDTH = DIFF_HEADS * 2 * HEAD_DIM
DIFF_V_DIM = 2 * HEAD_DIM
DIFF_WIDTH = DIFF_HEADS * DIFF_V_DIM
DIL_HEADS = D_MODEL // HEAD_DIM
DIL_PATTERNS = ((128, 1), (512, 4), (2048, 16))
DIL_BLOCK = 128
D_FF = -(-8 * D_MODEL // (3 * 256)) * 256

LANES = 128
SUBLANES = 8
MXU_DIM = 256
VMEM_LIMIT_BYTES = 56 * 1024 * 1024

F32 = jnp.float32
BF16 = jnp.bfloat16


def _dot(a, b):
    return jnp.dot(a, b, preferred_element_type=F32)


def _dot_nt(a, b):
    return lax.dot_general(a, b, (((1,), (1,)), ((), ())), preferred_element_type=F32)


def _rms(xf, g):
    ms = jnp.mean(xf * xf, axis=-1, keepdims=True)
    return xf * lax.rsqrt(ms + NORM_EPS) * g


def _const_spec(shape):
    nd = len(shape)
    return pl.BlockSpec(shape, lambda *_: (0,) * nd, pipeline_mode=pl.Buffered(1))


def _head_group_norm_rope(t, head_mean, tab_c, tab_a, tab_b):
    inv = lax.rsqrt(_dot((t * t).astype(BF16), head_mean) + NORM_EPS)
    out = []
    for c in range(t.shape[1] // LANES):
        tc = t[:, c * LANES:(c + 1) * LANES]
        out.append(tc * tab_c + pltpu.roll(tc, ROT_HALF, 1) * tab_a
                   + pltpu.roll(tc, LANES - ROT_HALF, 1) * tab_b)
    return inv * jnp.concatenate(out, axis=1)


def _gained_tables(tables, gain, scale):
    cos_t, sin_a, sin_b = tables
    g2 = jnp.concatenate([gain, gain]).reshape(1, LANES).astype(F32)
    return (cos_t * g2 * scale, sin_a * jnp.roll(g2, ROT_HALF, axis=1) * scale,
            sin_b * jnp.roll(g2, -ROT_HALF, axis=1) * scale)


def _rope_tables(seq_len):
    lane = jnp.arange(LANES) % HEAD_DIM
    pos = jnp.arange(seq_len, dtype=F32)
    inv_freq = 1.0 / (ROPE_THETA ** ((2 * (lane % ROT_HALF)).astype(F32) / ROT_DIM))
    ang = pos[:, None] * inv_freq[None, :]
    cos, sin = jnp.cos(ang), jnp.sin(ang)
    cos_t = jnp.where(lane < ROT_DIM, cos, 1.0)
    sin_a = jnp.where((lane >= ROT_HALF) & (lane < ROT_DIM), sin, 0.0)
    sin_b = jnp.where(lane < ROT_HALF, -sin, 0.0)
    return cos_t, sin_a, sin_b


def _proj_kernel(x_ref, g_ref, w_ref, ones_ref, qc_ref, qa_ref, qb_ref, kc_ref, ka_ref, kb_ref,
                 *rest, plain_w, qk_w, v_w, v_transposed):
    h = _rms(x_ref[...], g_ref[...]).astype(BF16)
    if v_transposed:
        wvt_ref, *out_refs = rest
    else:
        out_refs = rest
    if plain_w:
        plain_ref, q_ref, k_ref, v_ref = out_refs
        plain_ref[...] = _dot(h, w_ref[:, 0:plain_w])
    else:
        q_ref, k_ref, v_ref = out_refs
    head_mean = ones_ref[...]
    per_group = MXU_DIM // LANES
    for ref, tabs, off in ((q_ref, (qc_ref, qa_ref, qb_ref), plain_w),
                           (k_ref, (kc_ref, ka_ref, kb_ref), plain_w + qk_w)):
        t_all = _dot(h, w_ref[:, off:off + qk_w])
        tab_c, tab_a, tab_b = (t[...] for t in tabs)
        for grp in range(qk_w // MXU_DIM):
            y = _head_group_norm_rope(t_all[:, grp * MXU_DIM:(grp + 1) * MXU_DIM], head_mean,
                                      tab_c, tab_a, tab_b).astype(BF16)
            for c in range(per_group):
                ref[grp * per_group + c] = y[:, c * LANES:(c + 1) * LANES]
    if v_transposed:
        v_ref[0] = _dot_nt(wvt_ref[...], h).astype(BF16)
    else:
        off = plain_w + 2 * qk_w
        v_all = _dot(h, w_ref[:, off:off + v_w]).astype(BF16)
        for c in range(v_w // LANES):
            v_ref[c] = v_all[:, c * LANES:(c + 1) * LANES]


def _project(x2, seq_len, norm_g, w, q_gain, k_gain, tables, *, plain_w, qk_w, v_w, tm,
             v_transposed=False):
    n, d = x2.shape
    n_out = plain_w + 2 * qk_w + v_w
    assert w.shape == (d, n_out) and n % tm == 0 and seq_len % tm == 0
    tiles_per_seq = seq_len // tm
    row = lambda i: (i, 0)
    pos = lambda i: (i % tiles_per_seq, 0)
    ones2 = jnp.kron(jnp.eye(MXU_DIM // HEAD_DIM, dtype=F32),
                     jnp.full((HEAD_DIM, HEAD_DIM), 1.0 / HEAD_DIM, F32)).astype(BF16)
    rope_in = (*_gained_tables(tables, q_gain, HEAD_DIM ** -0.5 * LOG2_E),
               *_gained_tables(tables, k_gain, 1.0))
    out_shape, out_specs = [], []
    if plain_w:
        out_shape.append(jax.ShapeDtypeStruct((n, plain_w), F32))
        out_specs.append(pl.BlockSpec((tm, plain_w), row))
    chunked = lambda width: (jax.ShapeDtypeStruct((width // LANES, n, LANES), BF16),
                             pl.BlockSpec((width // LANES, tm, LANES), lambda i: (0, i, 0)))
    for width in (qk_w, qk_w):
        shape, spec = chunked(width)
        out_shape.append(shape)
        out_specs.append(spec)
    operands = [x2, norm_g.reshape(1, d), w.astype(BF16), ones2, *rope_in]
    in_specs = [pl.BlockSpec((tm, d), row), _const_spec((1, d)), _const_spec((d, n_out)),
                _const_spec(ones2.shape)] + [pl.BlockSpec((tm, LANES), pos) for _ in rope_in]
    if v_transposed:
        operands.append(w[:, n_out - v_w:].T.astype(BF16))
        in_specs.append(_const_spec((v_w, d)))
        out_shape.append(jax.ShapeDtypeStruct((n // seq_len, v_w, seq_len), BF16))
        out_specs.append(pl.BlockSpec((1, v_w, tm),
                                      lambda i: (i // tiles_per_seq, 0, i % tiles_per_seq)))
    else:
        shape, spec = chunked(v_w)
        out_shape.append(shape)
        out_specs.append(spec)
    return pl.pallas_call(
        functools.partial(_proj_kernel, plain_w=plain_w, qk_w=qk_w, v_w=v_w,
                          v_transposed=v_transposed),
        grid=(n // tm,),
        in_specs=in_specs,
        out_specs=out_specs,
        out_shape=out_shape,
        compiler_params=pltpu.CompilerParams(dimension_semantics=("arbitrary",),
                                             vmem_limit_bytes=VMEM_LIMIT_BYTES),
        name="proj_qk_rope",
    )(*operands)


def _rglru_gates(x, cw_ref, cb_ref, wax_ref, tail_scr):
    ts = x.shape[0]
    xe = jnp.concatenate([tail_scr[...], x], axis=0)
    cw = cw_ref[...]
    u = cb_ref[...] + x * cw[CONV_WIDTH - 1:CONV_WIDTH]
    for back in range(1, CONV_WIDTH):
        shifted = pltpu.roll(xe, back, 0)[SUBLANES:]
        u = u + shifted * cw[CONV_WIDTH - 1 - back:CONV_WIDTH - back]
    tail_scr[...] = x[ts - SUBLANES:]

    ub = u.astype(BF16)
    half = RNN_WIDTH // 2
    ra0 = _dot(ub[:, :half], wax_ref[0])
    ra1 = _dot(ub[:, half:], wax_ref[1])
    pre_r = jnp.concatenate([ra0[:, :half], ra1[:, :half]], axis=-1)
    pre_i = jnp.concatenate([ra0[:, half:], ra1[:, half:]], axis=-1)
    return u, pre_r, pre_i


def _rglru_rows(u, pre_r, pre_i, gate, ba, bx, softplus, carry):
    rows = u.shape[0]
    r = jax.nn.sigmoid(pre_r + ba)
    i = jax.nn.sigmoid(pre_i + bx)
    log_a = (-RGLRU_C) * r * softplus
    a = jnp.exp(log_a)
    b = jnp.sqrt(-jnp.tanh(log_a) * (a * a + 1.0)) * (i * u)

    grouped = (rows // SUBLANES, SUBLANES, a.shape[1])
    a, b = a.reshape(grouped), b.reshape(grouped)
    row_in_group = lax.broadcasted_iota(jnp.int32, grouped, 1)
    step = 1
    while step < SUBLANES:
        keep = row_in_group >= step
        a_prev = jnp.where(keep, pltpu.roll(a, step, 1), 1.0)
        b_prev = jnp.where(keep, pltpu.roll(b, step, 1), 0.0)
        b = a * b_prev + b
        a = a * a_prev
        step *= 2
    a, b = a.reshape(rows, grouped[2]), b.reshape(rows, grouped[2])
    groups = []
    for grp in range(rows // SUBLANES):
        sl = slice(grp * SUBLANES, (grp + 1) * SUBLANES)
        h_grp = b[sl] + a[sl] * carry
        groups.append(h_grp)
        carry = h_grp[SUBLANES - 1:SUBLANES, :]
    h = jnp.concatenate(groups, axis=0)
    return (h * jax.nn.gelu(gate)).astype(BF16), carry


def _rglru_operands(conv_w, conv_b, wa, wx, ba, bx, lru_lambda):
    w = RNN_WIDTH
    half = w // 2
    per_half = RNN_BLOCKS // 2

    def block_diag(wt):
        blocks = wt.astype(F32).reshape(2, per_half, RNN_BLOCK, 1, RNN_BLOCK)
        keep = jnp.eye(per_half, dtype=bool).reshape(1, per_half, 1, per_half, 1)
        return jnp.where(keep, blocks, 0.0).reshape(2, half, half)

    wax = jnp.concatenate([block_diag(wa), block_diag(wx)], axis=-1).astype(BF16)
    vec = lambda v: v.reshape(1, w).astype(F32)
    operands = [conv_w.astype(F32), vec(conv_b), wax, vec(ba), vec(bx), vec(lru_lambda)]
    return operands, [_const_spec(op.shape) for op in operands]


def _diff_attn_kernel(q_ref, k_ref, v_ref, lq1_ref, lk1_ref, lq2_ref, lk2_ref, sg_ref, o_ref,
                      *, tq, tk, lambda_init):
    qi = pl.program_id(2)
    q = q_ref[...]
    lo = lax.broadcasted_iota(jnp.int32, q.shape, 1) < HEAD_DIM
    zero = jnp.zeros_like(q)
    q_parts = (jnp.where(lo, q, zero), jnp.where(lo, zero, q))
    q_pos = qi * tq + lax.broadcasted_iota(jnp.int32, (tq, tk), 0)
    k_off = lax.broadcasted_iota(jnp.int32, (tq, tk), 1)

    def body(j, carry):
        start = pl.multiple_of(j * tk, tk)
        kb = k_ref[pl.ds(start, tk), :]
        vb = v_ref[pl.ds(start, tk), :]
        visible = (k_off + start) <= q_pos
        new = []
        for c in range(2):
            m, l, acc = carry[c]
            s = jnp.where(visible, _dot_nt(q_parts[c], kb), NEG_INF)
            m_new = jnp.maximum(m, jnp.max(s, axis=-1, keepdims=True))
            alpha = jnp.exp2(m - m_new)
            e = jnp.exp2(s - m_new)
            l = alpha * l + jnp.sum(e, axis=-1, keepdims=True)
            acc = alpha * acc + _dot(e.astype(BF16), vb)
            new.append((m_new, l, acc))
        return tuple(new)

    init = tuple((jnp.full((tq, 1), NEG_INF, F32), jnp.zeros((tq, 1), F32),
                  jnp.zeros((tq, DIFF_V_DIM), F32)) for _ in range(2))
    n_kv = (qi * tq + tq + tk - 1) // tk
    (_, l0, acc0), (_, l1, acc1) = lax.fori_loop(0, n_kv, body, init)

    dot_l = lambda a, b: jnp.sum(a[...] * b[...], axis=-1, keepdims=True)
    lam = jnp.exp(dot_l(lq1_ref, lk1_ref)) - jnp.exp(dot_l(lq2_ref, lk2_ref)) + lambda_init
    o = acc0 / l0 - lam * (acc1 / l1)
    o_ref[...] = (_rms(o, sg_ref[...]) * (1.0 - lambda_init)).astype(BF16)


def _diff_fast_kernel(q_ref, k_ref, vt_ref, lq1_ref, lk1_ref, lq2_ref, lk2_ref, sg_ref, o_ref,
                      s_scr, *, seq_len, tq, lambda_init):
    dot_l = lambda a, b: jnp.sum(a[...] * b[...], axis=-1, keepdims=True)
    lam = jnp.exp(dot_l(lq1_ref, lk1_ref)) - jnp.exp(dot_l(lq2_ref, lk2_ref)) + lambda_init
    lo = lax.broadcasted_iota(jnp.int32, (tq, LANES), 1) < HEAD_DIM
    key_row = lax.broadcasted_iota(jnp.int32, (tq, 2 * tq), 0)
    q_col = lax.broadcasted_iota(jnp.int32, (tq, 2 * tq), 1) % tq
    causal = key_row <= q_col
    for i in range(seq_len // tq):
        qt = q_ref[i * tq:(i + 1) * tq, :]
        zero = jnp.zeros_like(qt)
        q2 = jnp.concatenate([jnp.where(lo, qt, zero), jnp.where(lo, zero, qt)], axis=0)
        n_keys = (i + 1) * tq
        s_scr[0:n_keys, :] = _dot_nt(k_ref[0:n_keys, :], q2)
        acc = jnp.zeros((DIFF_V_DIM, 2 * tq), F32)
        l8 = jnp.zeros((SUBLANES, 2 * tq), F32)
        for j in range(i + 1):
            s = s_scr[j * tq:(j + 1) * tq, :]
            if j == i:
                s = jnp.where(causal, s, NEG_INF)
            e = jnp.exp2(s)
            l8 = l8 + jnp.sum(e.reshape(tq // SUBLANES, SUBLANES, 2 * tq), axis=0)
            acc = acc + _dot(vt_ref[0, :, j * tq:(j + 1) * tq], e.astype(BF16))
        l = jnp.sum(l8, axis=0, keepdims=True)
        o = acc[:, :tq] / l[:, :tq] - lam * (acc[:, tq:] / l[:, tq:])
        ms = jnp.mean(o * o, axis=0, keepdims=True)
        y = o * lax.rsqrt(ms + NORM_EPS) * sg_ref[...] * (1.0 - lambda_init)
        o_ref[i * tq:(i + 1) * tq, :] = y.T.astype(BF16)


FAST_SCORE_BOUND = 60.0


def _diff_attention(q4, k4, vt3, lq1, lk1, lq2, lk2, subln_g, score_bound, *, layer_idx, tq, tk):
    _, bsz, seq_len, _ = q4.shape
    lambda_init = 0.8 - 0.6 * math.exp(-0.3 * layer_idx)
    vec = lambda v: v.reshape(1, -1).astype(F32)
    lam_ops = (vec(lq1), vec(lk1), vec(lq2), vec(lk2))
    lam_specs = [_const_spec((1, HEAD_DIM)) for _ in range(4)]
    seq_spec = pl.BlockSpec((None, None, seq_len, LANES), lambda b, h: (h, b, 0, 0))
    out_shape = jax.ShapeDtypeStruct((DIFF_HEADS, bsz, seq_len, LANES), BF16)

    def fast(q3, k3, vt3):
        gain_cols = jnp.broadcast_to(subln_g.astype(F32)[:, None], (DIFF_V_DIM, tq))
        return pl.pallas_call(
            functools.partial(_diff_fast_kernel, seq_len=seq_len, tq=tq, lambda_init=lambda_init),
            grid=(bsz, DIFF_HEADS),
            in_specs=[seq_spec, seq_spec, pl.BlockSpec((1, LANES, seq_len), lambda b, h: (b, h, 0)),
                      *lam_specs, _const_spec((DIFF_V_DIM, tq))],
            out_specs=seq_spec,
            out_shape=out_shape,
            scratch_shapes=[pltpu.VMEM((seq_len, 2 * tq), F32)],
            compiler_params=pltpu.CompilerParams(dimension_semantics=("arbitrary", "arbitrary"),
                                                 vmem_limit_bytes=VMEM_LIMIT_BYTES),
            name="diff_attn_fast",
        )(q3, k3, vt3, *lam_ops, gain_cols)

    def safe(q3, k3, vt3):
        kv_spec = pl.BlockSpec((None, None, seq_len, LANES), lambda b, h, i: (h, b, 0, 0))
        q_spec = pl.BlockSpec((None, None, tq, LANES), lambda b, h, i: (h, b, i, 0))
        v4 = vt3.reshape(bsz, DIFF_HEADS, DIFF_V_DIM, seq_len).transpose(1, 0, 3, 2)
        return pl.pallas_call(
            functools.partial(_diff_attn_kernel, tq=tq, tk=tk, lambda_init=lambda_init),
            grid=(bsz, DIFF_HEADS, seq_len // tq),
            in_specs=[q_spec, kv_spec, kv_spec, *lam_specs, _const_spec((1, DIFF_V_DIM))],
            out_specs=q_spec,
            out_shape=out_shape,
            compiler_params=pltpu.CompilerParams(
                dimension_semantics=("arbitrary", "arbitrary", "arbitrary"),
                vmem_limit_bytes=VMEM_LIMIT_BYTES),
            name="diff_attn",
        )(q3, k3, v4, *lam_ops, vec(subln_g))

    return lax.cond(score_bound <= FAST_SCORE_BOUND, fast, safe, q4, k4, vt3)


def _dil_unit(q0, q1, k2, va, vb, band):
    lo = lax.broadcasted_iota(jnp.int32, (DIL_BLOCK, LANES), 1) < HEAD_DIM
    res, maxes = [], []
    for qh, vh in ((q0, va), (q1, vb)):
        s = jnp.where(band, _dot_nt(qh, k2), NEG_INF)
        m = s[:, :LANES]
        for c in range(1, s.shape[1] // LANES):
            m = jnp.maximum(m, s[:, c * LANES:(c + 1) * LANES])
        m = jnp.max(m, axis=-1, keepdims=True)
        e = jnp.exp2(s - m).astype(BF16)
        res.append(_dot(e, vh))
        maxes.append(m)
    acc = jnp.where(lo, res[0], res[1])
    l = pltpu.roll(jnp.where(lo, res[1], res[0]), HEAD_DIM, 1)
    return acc, jnp.where(lo, maxes[0], maxes[1]), l


def _dil_attn_kernel(q_ref, k_ref, v_ref, o_ref, qf, kf, vf, q0c, q1c, kc, vac, vbc,
                     *nat, seq_len):
    blk = DIL_BLOCK
    n_pat = len(DIL_PATTERNS)
    acc_n, m_n, l_n = nat[:n_pat], nat[n_pat:2 * n_pat], nat[2 * n_pat:]
    qf[...] = q_ref[...].astype(F32)
    kf[...] = k_ref[...].astype(F32)
    vf[...] = v_ref[...].astype(F32)

    qi = lax.broadcasted_iota(jnp.int32, (blk, blk), 0)
    ki = lax.broadcasted_iota(jnp.int32, (blk, blk), 1)
    band_cur = ki <= qi
    band_two = jnp.concatenate([ki >= qi, band_cur], axis=1)

    for g, (window, dil) in enumerate(DIL_PATTERNS):
        assert window // dil == blk and seq_len % (blk * dil) == 0
        lc = seq_len // dil
        for r in range(dil):
            rows = pl.ds(r, lc, stride=dil) if dil > 1 else pl.ds(0, lc)
            qv, vv = qf[rows, :], vf[rows, :]
            lo = lax.broadcasted_iota(jnp.int32, qv.shape, 1) < HEAD_DIM
            q0c[0:lc, :] = jnp.where(lo, qv, 0.0).astype(BF16)
            q1c[0:lc, :] = jnp.where(lo, 0.0, qv).astype(BF16)
            kc[0:lc, :] = kf[rows, :].astype(BF16)
            vac[0:lc, :] = jnp.where(lo, vv, 1.0).astype(BF16)
            vbc[0:lc, :] = jnp.where(lo, 1.0, vv).astype(BF16)
            for n in range(lc // blk):
                cur = slice(n * blk, (n + 1) * blk)
                keys = slice(max(n - 1, 0) * blk, (n + 1) * blk)
                acc, m, l = _dil_unit(q0c[cur, :], q1c[cur, :], kc[keys, :], vac[keys, :],
                                      vbc[keys, :], band_two if n else band_cur)
                first = n * blk * dil + r
                dst = pl.ds(first, blk, stride=dil) if dil > 1 else pl.ds(first, blk)
                acc_n[g][dst, :] = acc
                m_n[g][dst, :] = m
                l_n[g][dst, :] = l

    chunk = 256
    for c in range(seq_len // chunk):
        sl = slice(c * chunk, (c + 1) * chunk)
        ms = [m_n[g][sl, :] for g in range(n_pat)]
        m_all = functools.reduce(jnp.maximum, ms)
        num = jnp.zeros((chunk, LANES), F32)
        den = jnp.zeros((chunk, LANES), F32)
        for g in range(n_pat):
            wgt = jnp.exp2(ms[g] - m_all)
            num = num + wgt * acc_n[g][sl, :]
            den = den + wgt * l_n[g][sl, :]
        o_ref[sl, :] = (num / den).astype(BF16)


DIL_PAIRS_PER_STEP = 2


def _dil_fast_kernel(q_ref, k_ref, v_ref, o_ref, bias_scr, *bufs, seq_len):
    blk = DIL_BLOCK
    qi = lax.broadcasted_iota(jnp.int32, (2 * blk, blk), 0) % blk
    ki = lax.broadcasted_iota(jnp.int32, (2 * blk, blk), 1)
    bias_scr[:, 0:blk] = jnp.where(ki >= qi, 0.0, NEG_INF)
    bias_scr[:, blk:] = jnp.where(ki <= qi, 0.0, NEG_INF)
    for pair in range(q_ref.shape[0]):
        _dil_fast_pair(q_ref.at[pair], k_ref.at[pair], v_ref.at[pair], o_ref.at[pair], bias_scr,
                       bufs, seq_len)


def _dil_fast_pair(q_ref, k_ref, v_ref, o_ref, bias_scr, bufs, seq_len):
    blk = DIL_BLOCK
    n_pat = len(DIL_PATTERNS)
    natural, staged = bufs[0:3], bufs[3:6]
    out_scr = bufs[6]
    cls = [bufs[7 + 5 * g:12 + 5 * g] for g in range(n_pat)]
    res = bufs[7 + 5 * n_pat:]
    acc_n, l_n = res[:n_pat], res[n_pat:]
    dilated = [dil for _, dil in DIL_PATTERNS if dil > 1]
    merge_dil = min(dilated) if dilated else 1
    assert all(dil % merge_dil == 0 for dil in dilated)

    lo_blk = lax.broadcasted_iota(jnp.int32, (blk, LANES), 1) < HEAD_DIM

    if any(dil > 1 for _, dil in DIL_PATTERNS):
        for dst, src in zip(natural, (q_ref, k_ref, v_ref)):
            dst[...] = src[...].astype(F32)

    staged_dil = 1
    for g, (window, dil) in enumerate(DIL_PATTERNS):
        assert window // dil == blk and seq_len % (blk * dil) == 0
        lc = seq_len // dil
        q0c, q1c, kc, vac, vbc = cls[g]
        refine = staged_dil > 1 and dil % staged_dil == 0
        stage_here = (dil > 1 and not refine
                      and any(d2 > dil and d2 % dil == 0 for _, d2 in DIL_PATTERNS[g + 1:]))
        for r in range(dil):
            base = r * lc
            if dil == 1:
                qv, kv, vv = q_ref[...], k_ref[...], v_ref[...]
            else:
                if refine:
                    first = (r % staged_dil) * (seq_len // staged_dil) + r // staged_dil
                    rows, srcs = pl.ds(first, lc, stride=dil // staged_dil), staged
                else:
                    rows, srcs = pl.ds(r, lc, stride=dil), natural
                qv, kv, vv = (s[rows, :] for s in srcs)
                if stage_here:
                    for dst, val in zip(staged, (qv, kv, vv)):
                        dst[base:base + lc, :] = val
            lo = lax.broadcasted_iota(jnp.int32, qv.shape, 1) < HEAD_DIM
            zero, one = jnp.zeros_like(qv), jnp.ones_like(vv)
            q0c[base:base + lc, :] = jnp.where(lo, qv, zero).astype(BF16)
            q1c[base:base + lc, :] = jnp.where(lo, zero, qv).astype(BF16)
            kc[base:base + lc, :] = kv.astype(BF16)
            vac[base:base + lc, :] = jnp.where(lo, vv, one).astype(BF16)
            vbc[base:base + lc, :] = jnp.where(lo, one, vv).astype(BF16)
            for n in range(lc // blk):
                cur = slice(base + n * blk, base + (n + 1) * blk)
                keys = slice(base + max(n - 1, 0) * blk, base + (n + 1) * blk)
                q2 = jnp.concatenate([q0c[cur, :], q1c[cur, :]], axis=0)
                bias = bias_scr[...] if n else bias_scr[:, blk:]
                e = jnp.exp2(_dot_nt(q2, kc[keys, :]) + bias).astype(BF16)
                ra = _dot(e[:blk], vac[keys, :])
                rb = _dot(e[blk:], vbc[keys, :])
                if dil == 1:
                    dst = pl.ds(n * blk, blk)
                else:
                    sub = dil // merge_dil
                    first = ((r % merge_dil) * (seq_len // merge_dil) + r // merge_dil
                             + n * blk * sub)
                    dst = pl.ds(first, blk, stride=sub) if sub > 1 else pl.ds(first, blk)
                acc_n[g][dst, :] = jnp.where(lo_blk, ra, rb)
                l_n[g][dst, :] = jnp.where(lo_blk, rb, ra)
        if stage_here:
            staged_dil = dil

    lcm = seq_len // merge_dil
    chunk = 256
    for r in range(merge_dil):
        for c in range(lcm // chunk):
            cm_rows = pl.ds(r * lcm + c * chunk, chunk)
            nat_rows = (pl.ds(r + c * chunk * merge_dil, chunk, stride=merge_dil)
                        if merge_dil > 1 else cm_rows)
            rows_of = lambda g: nat_rows if DIL_PATTERNS[g][1] == 1 else cm_rows
            num = functools.reduce(jnp.add, [acc_n[g][rows_of(g), :] for g in range(n_pat)])
            den = functools.reduce(jnp.add, [l_n[g][rows_of(g), :] for g in range(n_pat)])
            out_scr[nat_rows, :] = num / pltpu.roll(den, HEAD_DIM, 1)
    for c in range(seq_len // chunk):
        sl = slice(c * chunk, (c + 1) * chunk)
        o_ref[sl, :] = out_scr[sl, :].astype(BF16)


def _dilated_attention(q4, k4, v4, score_bound):
    n_pairs, bsz, seq_len, _ = q4.shape
    n_pat = len(DIL_PATTERNS)

    def call(body, pairs_per_step, scratch_shapes, name):
        assert n_pairs % pairs_per_step == 0
        lead = None if pairs_per_step == 1 else pairs_per_step
        spec = pl.BlockSpec((lead, None, seq_len, LANES), lambda b, h: (h, b, 0, 0))
        return pl.pallas_call(
            functools.partial(body, seq_len=seq_len),
            grid=(bsz, n_pairs // pairs_per_step),
            in_specs=[spec, spec, spec],
            out_specs=spec,
            out_shape=jax.ShapeDtypeStruct(q4.shape, BF16),
            scratch_shapes=scratch_shapes,
            compiler_params=pltpu.CompilerParams(dimension_semantics=("arbitrary", "arbitrary"),
                                                 vmem_limit_bytes=VMEM_LIMIT_BYTES),
            name=name)

    seq_f32 = lambda count: [pltpu.VMEM((seq_len, LANES), F32) for _ in range(count)]
    seq_bf16 = lambda count: [pltpu.VMEM((seq_len, LANES), BF16) for _ in range(count)]
    fast = call(_dil_fast_kernel, DIL_PAIRS_PER_STEP,
                [pltpu.VMEM((2 * DIL_BLOCK, 2 * DIL_BLOCK), F32)]
                + seq_f32(7) + seq_bf16(5 * n_pat) + seq_f32(2 * n_pat), "dilated_attn_fast")
    safe = call(_dil_attn_kernel, 1, seq_f32(3) + seq_bf16(5) + seq_f32(3 * n_pat),
                "dilated_attn")
    return lax.cond(score_bound <= FAST_SCORE_BOUND, fast, safe, q4, k4, v4)


FFN_SUB_ROWS = 256


def _mix_ffn_tile(x, ys, wo_ref, g_ref, wg_ref, wu_ref, wd_ref, side_work=(), n_split=1):
    rows = x.shape[0] // n_split
    subs = [slice(s * rows, (s + 1) * rows) for s in range(n_split)]
    x1s, hs = [], []
    for sub in subs:
        x1 = x[sub]
        off = 0
        for y in ys:
            width = y.shape[1]
            x1 = x1 + _dot(y[sub], wo_ref[off:off + width, :])
            off += width
        x1s.append(x1)
        hs.append(_rms(x1, g_ref[...]).astype(BF16))
    n_chunks = wg_ref.shape[1] // MXU_DIM
    slots = n_chunks * n_split
    acts = [[] for _ in subs]
    slot = 0
    for c in range(n_chunks):
        cols = slice(c * MXU_DIM, (c + 1) * MXU_DIM)
        for s in range(n_split):
            gate = _dot(hs[s], wg_ref[:, cols])
            up = _dot(hs[s], wu_ref[:, cols])
            acts[s].append((gate * jax.nn.sigmoid(gate) * up).astype(BF16))
            for work in side_work[slot * len(side_work) // slots:
                                  (slot + 1) * len(side_work) // slots]:
                work(gate[0:1, :])
            slot += 1
    outs = [x1s[s] + _dot(jnp.concatenate(acts[s], axis=-1), wd_ref[...]) for s in range(n_split)]
    return outs[0] if n_split == 1 else jnp.concatenate(outs, axis=0)


def _chunks_to_lanes(y_ref):
    return jnp.concatenate([y_ref[c] for c in range(y_ref.shape[0])], axis=-1)


def _mix_ffn_kernel(x_ref, y_ref, wo_ref, g_ref, wg_ref, wu_ref, wd_ref, o_ref):
    o_ref[...] = _mix_ffn_tile(x_ref[...], [_chunks_to_lanes(y_ref)], wo_ref, g_ref, wg_ref,
                               wu_ref, wd_ref, n_split=x_ref.shape[0] // FFN_SUB_ROWS)


RGLRU_PIECE_ROWS = 16


def _rglru_mix_ffn_kernel(x_ref, xr_ref, gate_ref, yd_ref, cw_ref, cb_ref, wax_ref, ba_ref, bx_ref,
                          lam_ref, wo_ref, g_ref, wg_ref, wu_ref, wd_ref, o_ref,
                          tail_scr, h_scr, y_scr, *, tiles_per_seq):
    j = pl.program_id(0)

    @pl.when(j == 0)
    def _():
        y_scr[...] = jnp.zeros_like(y_scr)

    @pl.when(j % tiles_per_seq == 0)
    def _():
        tail_scr[...] = jnp.zeros_like(tail_scr)
        h_scr[...] = jnp.zeros_like(h_scr)

    y_prev = y_scr[...]
    u, pre_r, pre_i = _rglru_gates(xr_ref[...], cw_ref, cb_ref, wax_ref, tail_scr)
    z = -lam_ref[...]
    softplus = jnp.maximum(z, 0.0) + jnp.log1p(jnp.exp(-jnp.abs(z)))
    ba, bx = ba_ref[...], bx_ref[...]
    state = [h_scr[0:1, :]]

    def piece(p):
        def run(anchor):
            floor = jnp.minimum(jnp.concatenate([anchor] * (RNN_WIDTH // anchor.shape[1]), axis=1),
                                -3.0e38)
            sl = slice(p * RGLRU_PIECE_ROWS, (p + 1) * RGLRU_PIECE_ROWS)
            y_rows, state[0] = _rglru_rows(u[sl], jnp.maximum(pre_r[sl], floor),
                                           jnp.maximum(pre_i[sl], floor), gate_ref[sl, :], ba, bx,
                                           softplus, state[0])
            y_scr[sl, :] = y_rows
        return run

    n_pieces = xr_ref.shape[0] // RGLRU_PIECE_ROWS
    o_ref[...] = _mix_ffn_tile(x_ref[...], [y_prev, _chunks_to_lanes(yd_ref)], wo_ref, g_ref,
                               wg_ref, wu_ref, wd_ref,
                               side_work=[piece(p) for p in range(n_pieces)],
                               n_split=x_ref.shape[0] // FFN_SUB_ROWS)
    h_scr[...] = jnp.broadcast_to(state[0], h_scr.shape)


def _ffn_operands(w_out, norm_g, w_gate, w_up, w_down):
    d = norm_g.shape[0]
    operands = [w_out.astype(BF16), norm_g.reshape(1, d), w_gate.astype(BF16), w_up.astype(BF16),
                w_down.astype(BF16)]
    return operands, [_const_spec(op.shape) for op in operands]


def _mix_ffn(x2, y, w_out, norm_g, w_gate, w_up, w_down, *, tm):
    n, d = x2.shape
    row = lambda i: (i, 0)
    ffn_ops, ffn_specs = _ffn_operands(w_out, norm_g, w_gate, w_up, w_down)
    return pl.pallas_call(
        _mix_ffn_kernel,
        grid=(n // tm,),
        in_specs=[pl.BlockSpec((tm, d), row),
                  pl.BlockSpec((y.shape[0], tm, LANES), lambda i: (0, i, 0)), *ffn_specs],
        out_specs=pl.BlockSpec((tm, d), row),
        out_shape=jax.ShapeDtypeStruct((n, d), F32),
        compiler_params=pltpu.CompilerParams(dimension_semantics=("arbitrary",),
                                             vmem_limit_bytes=VMEM_LIMIT_BYTES),
        name="outproj_swiglu",
    )(x2, y, *ffn_ops)


def _rglru_mix_ffn(x2, xg, y_diff, seq_len, rglru_params, w_out, norm_g, w_gate, w_up, w_down,
                   *, tm):
    n, d = x2.shape
    w = RNN_WIDTH
    n_tiles = n // tm
    assert seq_len % tm == 0
    prev = lambda j: jnp.maximum(j - 1, 0)
    this = lambda j: jnp.minimum(j, n_tiles - 1)
    rg_ops, rg_specs = _rglru_operands(*rglru_params)
    ffn_ops, ffn_specs = _ffn_operands(w_out, norm_g, w_gate, w_up, w_down)
    return pl.pallas_call(
        functools.partial(_rglru_mix_ffn_kernel, tiles_per_seq=seq_len // tm),
        grid=(n_tiles + 1,),
        in_specs=[pl.BlockSpec((tm, d), lambda j: (prev(j), 0)),
                  pl.BlockSpec((tm, w), lambda j: (this(j), 0)),
                  pl.BlockSpec((tm, w), lambda j: (this(j), 1)),
                  pl.BlockSpec((y_diff.shape[0], tm, LANES), lambda j: (0, prev(j), 0)),
                  *rg_specs, *ffn_specs],
        out_specs=pl.BlockSpec((tm, d), lambda j: (prev(j), 0)),
        out_shape=jax.ShapeDtypeStruct((n, d), F32),
        scratch_shapes=[pltpu.VMEM((SUBLANES, w), F32), pltpu.VMEM((SUBLANES, w), F32),
                        pltpu.VMEM((tm, w), BF16)],
        compiler_params=pltpu.CompilerParams(dimension_semantics=("arbitrary",),
                                             vmem_limit_bytes=VMEM_LIMIT_BYTES),
        name="rglru_outproj_swiglu",
    )(x2, xg, xg, y_diff, *rg_ops, *ffn_ops)


def kernel(x, ab_norm_g, ab_w_in, ab_conv_w, ab_conv_b, ab_wa, ab_ba, ab_wx, ab_bx, ab_lru_lambda, ab_q_norm_g, ab_k_norm_g, ab_lambda_q1, ab_lambda_k1, ab_lambda_q2, ab_lambda_k2, ab_subln_g, ab_w_out, c_norm_g, c_w_qkv, c_q_norm_g, c_k_norm_g, c_w_out, ffn_norm_g, ffn_w_gate, ffn_w_up, ffn_w_down):
    bsz, seq_len, d = x.shape
    n = bsz * seq_len
    tables = _rope_tables(seq_len)
    x2 = x.reshape(n, d)

    xg, q, k, vt = _project(x2, seq_len, ab_norm_g[0], ab_w_in[0], ab_q_norm_g[0], ab_k_norm_g[0],
                            tables, plain_w=2 * RNN_WIDTH, qk_w=DIFF_QK_WIDTH, v_w=DIFF_WIDTH,
                            tm=1024, v_transposed=True)
    to4 = lambda t: t.reshape(-1, bsz, seq_len, LANES)
    rows = lambda t: t.reshape(-1, n, LANES)
    score_bound = lambda gq, gk: (HEAD_DIM ** 0.5) * jnp.max(jnp.abs(gq)) * jnp.max(jnp.abs(gk))
    y_diff = _diff_attention(to4(q), to4(k), vt, ab_lambda_q1[0], ab_lambda_k1[0],
                             ab_lambda_q2[0], ab_lambda_k2[0], ab_subln_g[0],
                             score_bound(ab_q_norm_g[0], ab_k_norm_g[0]),
                             layer_idx=0, tq=256, tk=256)
    rglru_params = (ab_conv_w[0], ab_conv_b[0], ab_wa[0], ab_wx[0], ab_ba[0], ab_bx[0],
                    ab_lru_lambda[0])
    x2 = _rglru_mix_ffn(x2, xg, rows(y_diff), seq_len, rglru_params, ab_w_out[0], ffn_norm_g[0],
                        ffn_w_gate[0], ffn_w_up[0], ffn_w_down[0], tm=512)

    q, k, v = _project(x2, seq_len, c_norm_g[0], c_w_qkv[0], c_q_norm_g[0], c_k_norm_g[0],
                       tables, plain_w=0, qk_w=D_MODEL, v_w=D_MODEL, tm=1024)
    o = _dilated_attention(to4(q), to4(k), to4(v), score_bound(c_q_norm_g[0], c_k_norm_g[0]))
    x2 = _mix_ffn(x2, rows(o), c_w_out[0], ffn_norm_g[1], ffn_w_gate[1], ffn_w_up[1],
                  ffn_w_down[1], tm=512)
    return x2.reshape(bsz, seq_len, d)
```

```python
import functools
import math

import jax
import jax.numpy as jnp
from jax import lax
from jax.experimental import pallas as pl
from jax.experimental.pallas import tpu as pltpu

D_MODEL = 1024
NORM_EPS = 1e-6
ROPE_THETA = 500000.0
HEAD_DIM = 64
ROT_DIM = HEAD_DIM // 4
ROT_HALF = ROT_DIM // 2
NEG_INF = -1e30
LOG2_E = math.log2(math.e)
RNN_WIDTH = D_MODEL // 2
RNN_BLOCKS = 8
RNN_BLOCK = RNN_WIDTH // RNN_BLOCKS
CONV_WIDTH = 4
RGLRU_C = 8.0
DIFF_HEADS = (D_MODEL // 2) // (2 * HEAD_DIM)
DIFF_QK_WIDTH = DIFF_HEADS * 2 * HEAD_DIM
DIFF_V_DIM = 2 * HEAD_DIM
DIFF_WIDTH = DIFF_HEADS * DIFF_V_DIM
DIL_HEADS = D_MODEL // HEAD_DIM
DIL_PATTERNS = ((128, 1), (512, 4), (2048, 16))
DIL_BLOCK = 128
D_FF = -(-8 * D_MODEL // (3 * 256)) * 256

LANES = 128
SUBLANES = 8
MXU_DIM = 256
VMEM_LIMIT_BYTES = 56 * 1024 * 1024

F32 = jnp.float32
BF16 = jnp.bfloat16


def _dot(a, b):
    return jnp.dot(a, b, preferred_element_type=F32)


def _dot_nt(a, b):
    return lax.dot_general(a, b, (((1,), (1,)), ((), ())), preferred_element_type=F32)


def _rms(xf, g):
    ms = jnp.mean(xf * xf, axis=-1, keepdims=True)
    return xf * lax.rsqrt(ms + NORM_EPS) * g


def _const_spec(shape):
    nd = len(shape)
    return pl.BlockSpec(shape, lambda *_: (0,) * nd, pipeline_mode=pl.Buffered(1))


def _head_group_norm_rope(t, head_mean, tab_c, tab_a, tab_b):
    inv = lax.rsqrt(_dot((t * t).astype(BF16), head_mean) + NORM_EPS)
    out = []
    for c in range(t.shape[1] // LANES):
        tc = t[:, c * LANES:(c + 1) * LANES]
        out.append(tc * tab_c + pltpu.roll(tc, ROT_HALF, 1) * tab_a
                   + pltpu.roll(tc, LANES - ROT_HALF, 1) * tab_b)
    return inv * jnp.concatenate(out, axis=1)


def _gained_tables(tables, gain, scale):
    cos_t, sin_a, sin_b = tables
    g2 = jnp.concatenate([gain, gain]).reshape(1, LANES).astype(F32)
    return (cos_t * g2 * scale, sin_a * jnp.roll(g2, ROT_HALF, axis=1) * scale,
            sin_b * jnp.roll(g2, -ROT_HALF, axis=1) * scale)


def _rope_tables(seq_len):
    lane = jnp.arange(LANES) % HEAD_DIM
    pos = jnp.arange(seq_len, dtype=F32)
    inv_freq = 1.0 / (ROPE_THETA ** ((2 * (lane % ROT_HALF)).astype(F32) / ROT_DIM))
    ang = pos[:, None] * inv_freq[None, :]
    cos, sin = jnp.cos(ang), jnp.sin(ang)
    cos_t = jnp.where(lane < ROT_DIM, cos, 1.0)
    sin_a = jnp.where((lane >= ROT_HALF) & (lane < ROT_DIM), sin, 0.0)
    sin_b = jnp.where(lane < ROT_HALF, -sin, 0.0)
    return cos_t, sin_a, sin_b


def _proj_kernel(x_ref, g_ref, w_ref, ones_ref, qc_ref, qa_ref, qb_ref, kc_ref, ka_ref, kb_ref,
                 *rest, plain_w, qk_w, v_w, v_transposed):
    h = _rms(x_ref[...], g_ref[...]).astype(BF16)
    if v_transposed:
        wvt_ref, *out_refs = rest
    else:
        out_refs = rest
    if plain_w:
        plain_ref, q_ref, k_ref, v_ref = out_refs
        plain_ref[...] = _dot(h, w_ref[:, 0:plain_w])
    else:
        q_ref, k_ref, v_ref = out_refs
    head_mean = ones_ref[...]
    per_group = MXU_DIM // LANES
    for ref, tabs, off in ((q_ref, (qc_ref, qa_ref, qb_ref), plain_w),
                           (k_ref, (kc_ref, ka_ref, kb_ref), plain_w + qk_w)):
        t_all = _dot(h, w_ref[:, off:off + qk_w])
        tab_c, tab_a, tab_b = (t[...] for t in tabs)
        for grp in range(qk_w // MXU_DIM):
            y = _head_group_norm_rope(t_all[:, grp * MXU_DIM:(grp + 1) * MXU_DIM], head_mean,
                                      tab_c, tab_a, tab_b).astype(BF16)
            for c in range(per_group):
                ref[grp * per_group + c] = y[:, c * LANES:(c + 1) * LANES]
    if v_transposed:
        v_ref[0] = _dot_nt(wvt_ref[...], h).astype(BF16)
    else:
        off = plain_w + 2 * qk_w
        v_all = _dot(h, w_ref[:, off:off + v_w]).astype(BF16)
        for c in range(v_w // LANES):
            v_ref[c] = v_all[:, c * LANES:(c + 1) * LANES]


def _project(x2, seq_len, norm_g, w, q_gain, k_gain, tables, *, plain_w, qk_w, v_w, tm,
             v_transposed=False):
    n, d = x2.shape
    n_out = plain_w + 2 * qk_w + v_w
    assert w.shape == (d, n_out) and n % tm == 0 and seq_len % tm == 0
    tiles_per_seq = seq_len // tm
    row = lambda i: (i, 0)
    pos = lambda i: (i % tiles_per_seq, 0)
    ones2 = jnp.kron(jnp.eye(MXU_DIM // HEAD_DIM, dtype=F32),
                     jnp.full((HEAD_DIM, HEAD_DIM), 1.0 / HEAD_DIM, F32)).astype(BF16)
    rope_in = (*_gained_tables(tables, q_gain, HEAD_DIM ** -0.5 * LOG2_E),
               *_gained_tables(tables, k_gain, 1.0))
    out_shape, out_specs = [], []
    if plain_w:
        out_shape.append(jax.ShapeDtypeStruct((n, plain_w), F32))
        out_specs.append(pl.BlockSpec((tm, plain_w), row))
    chunked = lambda width: (jax.ShapeDtypeStruct((width // LANES, n, LANES), BF16),
                             pl.BlockSpec((width // LANES, tm, LANES), lambda i: (0, i, 0)))
    for width in (qk_w, qk_w):
        shape, spec = chunked(width)
        out_shape.append(shape)
        out_specs.append(spec)
    operands = [x2, norm_g.reshape(1, d), w.astype(BF16), ones2, *rope_in]
    in_specs = [pl.BlockSpec((tm, d), row), _const_spec((1, d)), _const_spec((d, n_out)),
                _const_spec(ones2.shape)] + [pl.BlockSpec((tm, LANES), pos) for _ in rope_in]
    if v_transposed:
        operands.append(w[:, n_out - v_w:].T.astype(BF16))
        in_specs.append(_const_spec((v_w, d)))
        out_shape.append(jax.ShapeDtypeStruct((n // seq_len, v_w, seq_len), BF16))
        out_specs.append(pl.BlockSpec((1, v_w, tm),
                                      lambda i: (i // tiles_per_seq, 0, i % tiles_per_seq)))
    else:
        shape, spec = chunked(v_w)
        out_shape.append(shape)
        out_specs.append(spec)
    return pl.pallas_call(
        functools.partial(_proj_kernel, plain_w=plain_w, qk_w=qk_w, v_w=v_w,
                          v_transposed=v_transposed),
        grid=(n // tm,),
        in_specs=in_specs,
        out_specs=out_specs,
        out_shape=out_shape,
        compiler_params=pltpu.CompilerParams(dimension_semantics=("arbitrary",),
                                             vmem_limit_bytes=VMEM_LIMIT_BYTES),
        name="proj_qk_rope",
    )(*operands)


def _rglru_gates(x, cw_ref, cb_ref, wax_ref, tail_scr):
    ts = x.shape[0]
    xe = jnp.concatenate([tail_scr[...], x], axis=0)
    cw = cw_ref[...]
    u = cb_ref[...] + x * cw[CONV_WIDTH - 1:CONV_WIDTH]
    for back in range(1, CONV_WIDTH):
        shifted = pltpu.roll(xe, back, 0)[SUBLANES:]
        u = u + shifted * cw[CONV_WIDTH - 1 - back:CONV_WIDTH - back]
    tail_scr[...] = x[ts - SUBLANES:]

    ub = u.astype(BF16)
    half = RNN_WIDTH // 2
    ra0 = _dot(ub[:, :half], wax_ref[0])
    ra1 = _dot(ub[:, half:], wax_ref[1])
    pre_r = jnp.concatenate([ra0[:, :half], ra1[:, :half]], axis=-1)
    pre_i = jnp.concatenate([ra0[:, half:], ra1[:, half:]], axis=-1)
    return u, pre_r, pre_i


def _rglru_rows(u, pre_r, pre_i, gate, ba, bx, softplus, carry):
    rows = u.shape[0]
    r = jax.nn.sigmoid(pre_r + ba)
    i = jax.nn.sigmoid(pre_i + bx)
    log_a = (-RGLRU_C) * r * softplus
    a = jnp.exp(log_a)
    b = jnp.sqrt(-jnp.tanh(log_a) * (a * a + 1.0)) * (i * u)

    grouped = (rows // SUBLANES, SUBLANES, a.shape[1])
    a, b = a.reshape(grouped), b.reshape(grouped)
    row_in_group = lax.broadcasted_iota(jnp.int32, grouped, 1)
    step = 1
    while step < SUBLANES:
        keep = row_in_group >= step
        a_prev = jnp.where(keep, pltpu.roll(a, step, 1), 1.0)
        b_prev = jnp.where(keep, pltpu.roll(b, step, 1), 0.0)
        b = a * b_prev + b
        a = a * a_prev
        step *= 2
    a, b = a.reshape(rows, grouped[2]), b.reshape(rows, grouped[2])
    groups = []
    for grp in range(rows // SUBLANES):
        sl = slice(grp * SUBLANES, (grp + 1) * SUBLANES)
        h_grp = b[sl] + a[sl] * carry
        groups.append(h_grp)
        carry = h_grp[SUBLANES - 1:SUBLANES, :]
    h = jnp.concatenate(groups, axis=0)
    return (h * jax.nn.gelu(gate)).astype(BF16), carry


def _rglru_operands(conv_w, conv_b, wa, wx, ba, bx, lru_lambda):
    w = RNN_WIDTH
    half = w // 2
    per_half = RNN_BLOCKS // 2

    def block_diag(wt):
        blocks = wt.astype(F32).reshape(2, per_half, RNN_BLOCK, 1, RNN_BLOCK)
        keep = jnp.eye(per_half, dtype=bool).reshape(1, per_half, 1, per_half, 1)
        return jnp.where(keep, blocks, 0.0).reshape(2, half, half)

    wax = jnp.concatenate([block_diag(wa), block_diag(wx)], axis=-1).astype(BF16)
    vec = lambda v: v.reshape(1, w).astype(F32)
    operands = [conv_w.astype(F32), vec(conv_b), wax, vec(ba), vec(bx), vec(lru_lambda)]
    return operands, [_const_spec(op.shape) for op in operands]


def _diff_attn_kernel(q_ref, k_ref, v_ref, lq1_ref, lk1_ref, lq2_ref, lk2_ref, sg_ref, o_ref,
                      *, tq, tk, lambda_init):
    qi = pl.program_id(2)
    q = q_ref[...]
    lo = lax.broadcasted_iota(jnp.int32, q.shape, 1) < HEAD_DIM
    zero = jnp.zeros_like(q)
    q_parts = (jnp.where(lo, q, zero), jnp.where(lo, zero, q))
    q_pos = qi * tq + lax.broadcasted_iota(jnp.int32, (tq, tk), 0)
    k_off = lax.broadcasted_iota(jnp.int32, (tq, tk), 1)

    def body(j, carry):
        start = pl.multiple_of(j * tk, tk)
        kb = k_ref[pl.ds(start, tk), :]
        vb = v_ref[pl.ds(start, tk), :]
        visible = (k_off + start) <= q_pos
        new = []
        for c in range(2):
            m, l, acc = carry[c]
            s = jnp.where(visible, _dot_nt(q_parts[c], kb), NEG_INF)
            m_new = jnp.maximum(m, jnp.max(s, axis=-1, keepdims=True))
            alpha = jnp.exp2(m - m_new)
            e = jnp.exp2(s - m_new)
            l = alpha * l + jnp.sum(e, axis=-1, keepdims=True)
            acc = alpha * acc + _dot(e.astype(BF16), vb)
            new.append((m_new, l, acc))
        return tuple(new)

    init = tuple((jnp.full((tq, 1), NEG_INF, F32), jnp.zeros((tq, 1), F32),
                  jnp.zeros((tq, DIFF_V_DIM), F32)) for _ in range(2))
    n_kv = (qi * tq + tq + tk - 1) // tk
    (_, l0, acc0), (_, l1, acc1) = lax.fori_loop(0, n_kv, body, init)

    dot_l = lambda a, b: jnp.sum(a[...] * b[...], axis=-1, keepdims=True)
    lam = jnp.exp(dot_l(lq1_ref, lk1_ref)) - jnp.exp(dot_l(lq2_ref, lk2_ref)) + lambda_init
    o = acc0 / l0 - lam * (acc1 / l1)
    o_ref[...] = (_rms(o, sg_ref[...]) * (1.0 - lambda_init)).astype(BF16)


def _diff_fast_kernel(q_ref, k_ref, vt_ref, lq1_ref, lk1_ref, lq2_ref, lk2_ref, sg_ref, o_ref,
                      s_scr, *, seq_len, tq, lambda_init):
    dot_l = lambda a, b: jnp.sum(a[...] * b[...], axis=-1, keepdims=True)
    lam = jnp.exp(dot_l(lq1_ref, lk1_ref)) - jnp.exp(dot_l(lq2_ref, lk2_ref)) + lambda_init
    lo = lax.broadcasted_iota(jnp.int32, (tq, LANES), 1) < HEAD_DIM
    key_row = lax.broadcasted_iota(jnp.int32, (tq, 2 * tq), 0)
    q_col = lax.broadcasted_iota(jnp.int32, (tq, 2 * tq), 1) % tq
    causal = key_row <= q_col
    for i in range(seq_len // tq):
        qt = q_ref[i * tq:(i + 1) * tq, :]
        zero = jnp.zeros_like(qt)
        q2 = jnp.concatenate([jnp.where(lo, qt, zero), jnp.where(lo, zero, qt)], axis=0)
        n_keys = (i + 1) * tq
        s_scr[0:n_keys, :] = _dot_nt(k_ref[0:n_keys, :], q2)
        acc = jnp.zeros((DIFF_V_DIM, 2 * tq), F32)
        l8 = jnp.zeros((SUBLANES, 2 * tq), F32)
        for j in range(i + 1):
            s = s_scr[j * tq:(j + 1) * tq, :]
            if j == i:
                s = jnp.where(causal, s, NEG_INF)
            e = jnp.exp2(s)
            l8 = l8 + jnp.sum(e.reshape(tq // SUBLANES, SUBLANES, 2 * tq), axis=0)
            acc = acc + _dot(vt_ref[0, :, j * tq:(j + 1) * tq], e.astype(BF16))
        l = jnp.sum(l8, axis=0, keepdims=True)
        o = acc[:, :tq] / l[:, :tq] - lam * (acc[:, tq:] / l[:, tq:])
        ms = jnp.mean(o * o, axis=0, keepdims=True)
        y = o * lax.rsqrt(ms + NORM_EPS) * sg_ref[...] * (1.0 - lambda_init)
        o_ref[i * tq:(i + 1) * tq, :] = y.T.astype(BF16)


FAST_SCORE_BOUND = 60.0


def _diff_attention(q4, k4, vt3, lq1, lk1, lq2, lk2, subln_g, score_bound, *, layer_idx, tq, tk):
    _, bsz, seq_len, _ = q4.shape
    lambda_init = 0.8 - 0.6 * math.exp(-0.3 * layer_idx)
    vec = lambda v: v.reshape(1, -1).astype(F32)
    lam_ops = (vec(lq1), vec(lk1), vec(lq2), vec(lk2))
    lam_specs = [_const_spec((1, HEAD_DIM)) for _ in range(4)]
    seq_spec = pl.BlockSpec((None, None, seq_len, LANES), lambda b, h: (h, b, 0, 0))
    out_shape = jax.ShapeDtypeStruct((DIFF_HEADS, bsz, seq_len, LANES), BF16)

    def fast(q3, k3, vt3):
        gain_cols = jnp.broadcast_to(subln_g.astype(F32)[:, None], (DIFF_V_DIM, tq))
        return pl.pallas_call(
            functools.partial(_diff_fast_kernel, seq_len=seq_len, tq=tq, lambda_init=lambda_init),
            grid=(bsz, DIFF_HEADS),
            in_specs=[seq_spec, seq_spec, pl.BlockSpec((1, LANES, seq_len), lambda b, h: (b, h, 0)),
                      *lam_specs, _const_spec((DIFF_V_DIM, tq))],
            out_specs=seq_spec,
            out_shape=out_shape,
            scratch_shapes=[pltpu.VMEM((seq_len, 2 * tq), F32)],
            compiler_params=pltpu.CompilerParams(dimension_semantics=("arbitrary", "arbitrary"),
                                                 vmem_limit_bytes=VMEM_LIMIT_BYTES),
            name="diff_attn_fast",
        )(q3, k3, vt3, *lam_ops, gain_cols)

    def safe(q3, k3, vt3):
        kv_spec = pl.BlockSpec((None, None, seq_len, LANES), lambda b, h, i: (h, b, 0, 0))
        q_spec = pl.BlockSpec((None, None, tq, LANES), lambda b, h, i: (h, b, i, 0))
        v4 = vt3.reshape(bsz, DIFF_HEADS, DIFF_V_DIM, seq_len).transpose(1, 0, 3, 2)
        return pl.pallas_call(
            functools.partial(_diff_attn_kernel, tq=tq, tk=tk, lambda_init=lambda_init),
            grid=(bsz, DIFF_HEADS, seq_len // tq),
            in_specs=[q_spec, kv_spec, kv_spec, *lam_specs, _const_spec((1, DIFF_V_DIM))],
            out_specs=q_spec,
            out_shape=out_shape,
            compiler_params=pltpu.CompilerParams(
                dimension_semantics=("arbitrary", "arbitrary", "arbitrary"),
                vmem_limit_bytes=VMEM_LIMIT_BYTES),
            name="diff_attn",
        )(q3, k3, v4, *lam_ops, vec(subln_g))

    return lax.cond(score_bound <= FAST_SCORE_BOUND, fast, safe, q4, k4, vt3)


def _dil_unit(q0, q1, k2, va, vb, band):
    lo = lax.broadcasted_iota(jnp.int32, (DIL_BLOCK, LANES), 1) < HEAD_DIM
    res, maxes = [], []
    for qh, vh in ((q0, va), (q1, vb)):
        s = jnp.where(band, _dot_nt(qh, k2), NEG_INF)
        m = s[:, :LANES]
        for c in range(1, s.shape[1] // LANES):
            m = jnp.maximum(m, s[:, c * LANES:(c + 1) * LANES])
        m = jnp.max(m, axis=-1, keepdims=True)
        e = jnp.exp2(s - m).astype(BF16)
        res.append(_dot(e, vh))
        maxes.append(m)
    acc = jnp.where(lo, res[0], res[1])
    l = pltpu.roll(jnp.where(lo, res[1], res[0]), HEAD_DIM, 1)
    return acc, jnp.where(lo, maxes[0], maxes[1]), l


def _dil_attn_kernel(q_ref, k_ref, v_ref, o_ref, qf, kf, vf, q0c, q1c, kc, vac, vbc,
                     *nat, seq_len):
    blk = DIL_BLOCK
    n_pat = len(DIL_PATTERNS)
    acc_n, m_n, l_n = nat[:n_pat], nat[n_pat:2 * n_pat], nat[2 * n_pat:]
    qf[...] = q_ref[...].astype(F32)
    kf[...] = k_ref[...].astype(F32)
    vf[...] = v_ref[...].astype(F32)

    qi = lax.broadcasted_iota(jnp.int32, (blk, blk), 0)
    ki = lax.broadcasted_iota(jnp.int32, (blk, blk), 1)
    band_cur = ki <= qi
    band_two = jnp.concatenate([ki >= qi, band_cur], axis=1)

    for g, (window, dil) in enumerate(DIL_PATTERNS):
        assert window // dil == blk and seq_len % (blk * dil) == 0
        lc = seq_len // dil
        for r in range(dil):
            rows = pl.ds(r, lc, stride=dil) if dil > 1 else pl.ds(0, lc)
            qv, vv = qf[rows, :], vf[rows, :]
            lo = lax.broadcasted_iota(jnp.int32, qv.shape, 1) < HEAD_DIM
            q0c[0:lc, :] = jnp.where(lo, qv, 0.0).astype(BF16)
            q1c[0:lc, :] = jnp.where(lo, 0.0, qv).astype(BF16)
            kc[0:lc, :] = kf[rows, :].astype(BF16)
            vac[0:lc, :] = jnp.where(lo, vv, 1.0).astype(BF16)
            vbc[0:lc, :] = jnp.where(lo, 1.0, vv).astype(BF16)
            for n in range(lc // blk):
                cur = slice(n * blk, (n + 1) * blk)
                keys = slice(max(n - 1, 0) * blk, (n + 1) * blk)
                acc, m, l = _dil_unit(q0c[cur, :], q1c[cur, :], kc[keys, :], vac[keys, :],
                                      vbc[keys, :], band_two if n else band_cur)
                first = n * blk * dil + r
                dst = pl.ds(first, blk, stride=dil) if dil > 1 else pl.ds(first, blk)
                acc_n[g][dst, :] = acc
                m_n[g][dst, :] = m
                l_n[g][dst, :] = l

    chunk = 256
    for c in range(seq_len // chunk):
        sl = slice(c * chunk, (c + 1) * chunk)
        ms = [m_n[g][sl, :] for g in range(n_pat)]
        m_all = functools.reduce(jnp.maximum, ms)
        num = jnp.zeros((chunk, LANES), F32)
        den = jnp.zeros((chunk, LANES), F32)
        for g in range(n_pat):
            wgt = jnp.exp2(ms[g] - m_all)
            num = num + wgt * acc_n[g][sl, :]
            den = den + wgt * l_n[g][sl, :]
        o_ref[sl, :] = (num / den).astype(BF16)


DIL_PAIRS_PER_STEP = 1


def _dil_fast_kernel(q_ref, k_ref, v_ref, o_ref, bias_scr, *bufs, seq_len):
    blk = DIL_BLOCK
    qi = lax.broadcasted_iota(jnp.int32, (2 * blk, blk), 0) % blk
    ki = lax.broadcasted_iota(jnp.int32, (2 * blk, blk), 1)
    bias_scr[:, 0:blk] = jnp.where(ki >= qi, 0.0, NEG_INF)
    bias_scr[:, blk:] = jnp.where(ki <= qi, 0.0, NEG_INF)
    for pair in range(q_ref.shape[0]):
        _dil_fast_pair(q_ref.at[pair], k_ref.at[pair], v_ref.at[pair], o_ref.at[pair], bias_scr,
                       bufs, seq_len)


def _dil_fast_pair(q_ref, k_ref, v_ref, o_ref, bias_scr, bufs, seq_len):
    blk = DIL_BLOCK
    n_pat = len(DIL_PATTERNS)
    natural, staged = bufs[0:3], bufs[3:6]
    out_scr = bufs[6]
    cls = [bufs[7 + 5 * g:12 + 5 * g] for g in range(n_pat)]
    res = bufs[7 + 5 * n_pat:]
    acc_n, l_n = res[:n_pat], res[n_pat:]
    dilated = [dil for _, dil in DIL_PATTERNS if dil > 1]
    merge_dil = min(dilated) if dilated else 1
    assert all(dil % merge_dil == 0 for dil in dilated)

    lo_blk = lax.broadcasted_iota(jnp.int32, (blk, LANES), 1) < HEAD_DIM

    if any(dil > 1 for _, dil in DIL_PATTERNS):
        for dst, src in zip(natural, (q_ref, k_ref, v_ref)):
            dst[...] = src[...].astype(F32)

    staged_dil = 1
    for g, (window, dil) in enumerate(DIL_PATTERNS):
        assert window // dil == blk and seq_len % (blk * dil) == 0
        lc = seq_len // dil
        q0c, q1c, kc, vac, vbc = cls[g]
        refine = staged_dil > 1 and dil % staged_dil == 0
        stage_here = (dil > 1 and not refine
                      and any(d2 > dil and d2 % dil == 0 for _, d2 in DIL_PATTERNS[g + 1:]))
        for r in range(dil):
            base = r * lc
            if dil == 1:
                qv, kv, vv = q_ref[...], k_ref[...], v_ref[...]
            else:
                if refine:
                    first = (r % staged_dil) * (seq_len // staged_dil) + r // staged_dil
                    rows, srcs = pl.ds(first, lc, stride=dil // staged_dil), staged
                else:
                    rows, srcs = pl.ds(r, lc, stride=dil), natural
                qv, kv, vv = (s[rows, :] for s in srcs)
                if stage_here:
                    for dst, val in zip(staged, (qv, kv, vv)):
                        dst[base:base + lc, :] = val
            lo = lax.broadcasted_iota(jnp.int32, qv.shape, 1) < HEAD_DIM
            zero, one = jnp.zeros_like(qv), jnp.ones_like(vv)
            q0c[base:base + lc, :] = jnp.where(lo, qv, zero).astype(BF16)
            q1c[base:base + lc, :] = jnp.where(lo, zero, qv).astype(BF16)
            kc[base:base + lc, :] = kv.astype(BF16)
            vac[base:base + lc, :] = jnp.where(lo, vv, one).astype(BF16)
            vbc[base:base + lc, :] = jnp.where(lo, one, vv).astype(BF16)
            for n in range(lc // blk):
                cur = slice(base + n * blk, base + (n + 1) * blk)
                keys = slice(base + max(n - 1, 0) * blk, base + (n + 1) * blk)
                q2 = jnp.concatenate([q0c[cur, :], q1c[cur, :]], axis=0)
                bias = bias_scr[...] if n else bias_scr[:, blk:]
                e = jnp.exp2(_dot_nt(q2, kc[keys, :]) + bias).astype(BF16)
                ra = _dot(e[:blk], vac[keys, :])
                rb = _dot(e[blk:], vbc[keys, :])
                if dil == 1:
                    dst = pl.ds(n * blk, blk)
                else:
                    sub = dil // merge_dil
                    first = ((r % merge_dil) * (seq_len // merge_dil) + r // merge_dil
                             + n * blk * sub)
                    dst = pl.ds(first, blk, stride=sub) if sub > 1 else pl.ds(first, blk)
                acc_n[g][dst, :] = jnp.where(lo_blk, ra, rb)
                l_n[g][dst, :] = jnp.where(lo_blk, rb, ra)
        if stage_here:
            staged_dil = dil

    lcm = seq_len // merge_dil
    chunk = 256
    for r in range(merge_dil):
        for c in range(lcm // chunk):
            cm_rows = pl.ds(r * lcm + c * chunk, chunk)
            nat_rows = (pl.ds(r + c * chunk * merge_dil, chunk, stride=merge_dil)
                        if merge_dil > 1 else cm_rows)
            rows_of = lambda g: nat_rows if DIL_PATTERNS[g][1] == 1 else cm_rows
            num = functools.reduce(jnp.add, [acc_n[g][rows_of(g), :] for g in range(n_pat)])
            den = functools.reduce(jnp.add, [l_n[g][rows_of(g), :] for g in range(n_pat)])
            out_scr[nat_rows, :] = num / pltpu.roll(den, HEAD_DIM, 1)
    for c in range(seq_len // chunk):
        sl = slice(c * chunk, (c + 1) * chunk)
        o_ref[sl, :] = out_scr[sl, :].astype(BF16)


def _dilated_attention(q4, k4, v4, score_bound):
    n_pairs, bsz, seq_len, _ = q4.shape
    n_pat = len(DIL_PATTERNS)

    def call(body, pairs_per_step, scratch_shapes, name):
        lead = pairs_per_step
        pairs_per_step = pairs_per_step or 1
        assert n_pairs % pairs_per_step == 0
        spec = pl.BlockSpec((lead, None, seq_len, LANES), lambda b, h: (h, b, 0, 0))
        return pl.pallas_call(
            functools.partial(body, seq_len=seq_len),
            grid=(bsz, n_pairs // pairs_per_step),
            in_specs=[spec, spec, spec],
            out_specs=spec,
            out_shape=jax.ShapeDtypeStruct(q4.shape, BF16),
            scratch_shapes=scratch_shapes,
            compiler_params=pltpu.CompilerParams(dimension_semantics=("arbitrary", "arbitrary"),
                                                 vmem_limit_bytes=VMEM_LIMIT_BYTES),
            name=name)

    seq_f32 = lambda count: [pltpu.VMEM((seq_len, LANES), F32) for _ in range(count)]
    seq_bf16 = lambda count: [pltpu.VMEM((seq_len, LANES), BF16) for _ in range(count)]
    fast = call(_dil_fast_kernel, DIL_PAIRS_PER_STEP,
                [pltpu.VMEM((2 * DIL_BLOCK, 2 * DIL_BLOCK), F32)]
                + seq_f32(7) + seq_bf16(5 * n_pat) + seq_f32(2 * n_pat), "dilated_attn_fast")
    safe = call(_dil_attn_kernel, None, seq_f32(3) + seq_bf16(5) + seq_f32(3 * n_pat),
                "dilated_attn")
    return lax.cond(score_bound <= FAST_SCORE_BOUND, fast, safe, q4, k4, v4)


FFN_SUB_ROWS = 256


def _mix_ffn_tile(x, ys, wo_ref, g_ref, wg_ref, wu_ref, wd_ref, side_work=(), n_split=1):
    rows = x.shape[0] // n_split
    subs = [slice(s * rows, (s + 1) * rows) for s in range(n_split)]
    x1s, hs = [], []
    for sub in subs:
        x1 = x[sub]
        off = 0
        for y in ys:
            width = y.shape[1]
            x1 = x1 + _dot(y[sub], wo_ref[off:off + width, :])
            off += width
        x1s.append(x1)
        hs.append(_rms(x1, g_ref[...]).astype(BF16))
    n_chunks = wg_ref.shape[1] // MXU_DIM
    slots = n_chunks * n_split
    acts = [[] for _ in subs]
    slot = 0
    for c in range(n_chunks):
        cols = slice(c * MXU_DIM, (c + 1) * MXU_DIM)
        for s in range(n_split):
            gate = _dot(hs[s], wg_ref[:, cols])
            up = _dot(hs[s], wu_ref[:, cols])
            acts[s].append((gate * jax.nn.sigmoid(gate) * up).astype(BF16))
            for work in side_work[slot * len(side_work) // slots:
                                  (slot + 1) * len(side_work) // slots]:
                work(gate[0:1, :])
            slot += 1
    outs = [x1s[s] + _dot(jnp.concatenate(acts[s], axis=-1), wd_ref[...]) for s in range(n_split)]
    return outs[0] if n_split == 1 else jnp.concatenate(outs, axis=0)


def _chunks_to_lanes(y_ref):
    return jnp.concatenate([y_ref[c] for c in range(y_ref.shape[0])], axis=-1)


def _mix_ffn_kernel(x_ref, y_ref, wo_ref, g_ref, wg_ref, wu_ref, wd_ref, o_ref):
    o_ref[...] = _mix_ffn_tile(x_ref[...], [_chunks_to_lanes(y_ref)], wo_ref, g_ref, wg_ref,
                               wu_ref, wd_ref, n_split=x_ref.shape[0] // FFN_SUB_ROWS)


RGLRU_PIECE_ROWS = 16


def _rglru_mix_ffn_kernel(x_ref, xr_ref, gate_ref, yd_ref, cw_ref, cb_ref, wax_ref, ba_ref, bx_ref,
                          lam_ref, wo_ref, g_ref, wg_ref, wu_ref, wd_ref, o_ref,
                          tail_scr, h_scr, y_scr, *, tiles_per_seq):
    j = pl.program_id(0)

    @pl.when(j == 0)
    def _():
        y_scr[...] = jnp.zeros_like(y_scr)

    @pl.when(j % tiles_per_seq == 0)
    def _():
        tail_scr[...] = jnp.zeros_like(tail_scr)
        h_scr[...] = jnp.zeros_like(h_scr)

    y_prev = y_scr[...]
    u, pre_r, pre_i = _rglru_gates(xr_ref[...], cw_ref, cb_ref, wax_ref, tail_scr)
    z = -lam_ref[...]
    softplus = jnp.maximum(z, 0.0) + jnp.log1p(jnp.exp(-jnp.abs(z)))
    ba, bx = ba_ref[...], bx_ref[...]
    state = [h_scr[0:1, :]]

    def piece(p):
        def run(anchor):
            floor = jnp.minimum(jnp.concatenate([anchor] * (RNN_WIDTH // anchor.shape[1]), axis=1),
                                -3.0e38)
            sl = slice(p * RGLRU_PIECE_ROWS, (p + 1) * RGLRU_PIECE_ROWS)
            y_rows, state[0] = _rglru_rows(u[sl], jnp.maximum(pre_r[sl], floor),
                                           jnp.maximum(pre_i[sl], floor), gate_ref[sl, :], ba, bx,
                                           softplus, state[0])
            y_scr[sl, :] = y_rows
        return run

    n_pieces = xr_ref.shape[0] // RGLRU_PIECE_ROWS
    o_ref[...] = _mix_ffn_tile(x_ref[...], [y_prev, _chunks_to_lanes(yd_ref)], wo_ref, g_ref,
                               wg_ref, wu_ref, wd_ref,
                               side_work=[piece(p) for p in range(n_pieces)],
                               n_split=x_ref.shape[0] // FFN_SUB_ROWS)
    h_scr[...] = jnp.broadcast_to(state[0], h_scr.shape)


def _ffn_operands(w_out, norm_g, w_gate, w_up, w_down):
    d = norm_g.shape[0]
    operands = [w_out.astype(BF16), norm_g.reshape(1, d), w_gate.astype(BF16), w_up.astype(BF16),
                w_down.astype(BF16)]
    return operands, [_const_spec(op.shape) for op in operands]


def _mix_ffn(x2, y, w_out, norm_g, w_gate, w_up, w_down, *, tm):
    n, d = x2.shape
    row = lambda i: (i, 0)
    ffn_ops, ffn_specs = _ffn_operands(w_out, norm_g, w_gate, w_up, w_down)
    return pl.pallas_call(
        _mix_ffn_kernel,
        grid=(n // tm,),
        in_specs=[pl.BlockSpec((tm, d), row),
                  pl.BlockSpec((y.shape[0], tm, LANES), lambda i: (0, i, 0)), *ffn_specs],
        out_specs=pl.BlockSpec((tm, d), row),
        out_shape=jax.ShapeDtypeStruct((n, d), F32),
        compiler_params=pltpu.CompilerParams(dimension_semantics=("arbitrary",),
                                             vmem_limit_bytes=VMEM_LIMIT_BYTES),
        name="outproj_swiglu",
    )(x2, y, *ffn_ops)


def _rglru_mix_ffn(x2, xg, y_diff, seq_len, rglru_params, w_out, norm_g, w_gate, w_up, w_down,
                   *, tm):
    n, d = x2.shape
    w = RNN_WIDTH
    n_tiles = n // tm
    assert seq_len % tm == 0
    prev = lambda j: jnp.maximum(j - 1, 0)
    this = lambda j: jnp.minimum(j, n_tiles - 1)
    rg_ops, rg_specs = _rglru_operands(*rglru_params)
    ffn_ops, ffn_specs = _ffn_operands(w_out, norm_g, w_gate, w_up, w_down)
    return pl.pallas_call(
        functools.partial(_rglru_mix_ffn_kernel, tiles_per_seq=seq_len // tm),
        grid=(n_tiles + 1,),
        in_specs=[pl.BlockSpec((tm, d), lambda j: (prev(j), 0)),
                  pl.BlockSpec((tm, w), lambda j: (this(j), 0)),
                  pl.BlockSpec((tm, w), lambda j: (this(j), 1)),
                  pl.BlockSpec((y_diff.shape[0], tm, LANES), lambda j: (0, prev(j), 0)),
                  *rg_specs, *ffn_specs],
        out_specs=pl.BlockSpec((tm, d), lambda j: (prev(j), 0)),
        out_shape=jax.ShapeDtypeStruct((n, d), F32),
        scratch_shapes=[pltpu.VMEM((SUBLANES, w), F32), pltpu.VMEM((SUBLANES, w), F32),
                        pltpu.VMEM((tm, w), BF16)],
        compiler_params=pltpu.CompilerParams(dimension_semantics=("arbitrary",),
                                             vmem_limit_bytes=VMEM_LIMIT_BYTES),
        name="rglru_outproj_swiglu",
    )(x2, xg, xg, y_diff, *rg_ops, *ffn_ops)


def kernel(x, ab_norm_g, ab_w_in, ab_conv_w, ab_conv_b, ab_wa, ab_ba, ab_wx, ab_bx, ab_lru_lambda, ab_q_norm_g, ab_k_norm_g, ab_lambda_q1, ab_lambda_k1, ab_lambda_q2, ab_lambda_k2, ab_subln_g, ab_w_out, c_norm_g, c_w_qkv, c_q_norm_g, c_k_norm_g, c_w_out, ffn_norm_g, ffn_w_gate, ffn_w_up, ffn_w_down):
    bsz, seq_len, d = x.shape
    n = bsz * seq_len
    tables = _rope_tables(seq_len)
    x2 = x.reshape(n, d)

    xg, q, k, vt = _project(x2, seq_len, ab_norm_g[0], ab_w_in[0], ab_q_norm_g[0], ab_k_norm_g[0],
                            tables, plain_w=2 * RNN_WIDTH, qk_w=DIFF_QK_WIDTH, v_w=DIFF_WIDTH,
                            tm=1024, v_transposed=True)
    to4 = lambda t: t.reshape(-1, bsz, seq_len, LANES)
    rows = lambda t: t.reshape(-1, n, LANES)
    score_bound = lambda gq, gk: (HEAD_DIM ** 0.5) * jnp.max(jnp.abs(gq)) * jnp.max(jnp.abs(gk))
    y_diff = _diff_attention(to4(q), to4(k), vt, ab_lambda_q1[0], ab_lambda_k1[0],
                             ab_lambda_q2[0], ab_lambda_k2[0], ab_subln_g[0],
                             score_bound(ab_q_norm_g[0], ab_k_norm_g[0]),
                             layer_idx=0, tq=256, tk=256)
    rglru_params = (ab_conv_w[0], ab_conv_b[0], ab_wa[0], ab_wx[0], ab_ba[0], ab_bx[0],
                    ab_lru_lambda[0])
    x2 = _rglru_mix_ffn(x2, xg, rows(y_diff), seq_len, rglru_params, ab_w_out[0], ffn_norm_g[0],
                        ffn_w_gate[0], ffn_w_up[0], ffn_w_down[0], tm=512)

    q, k, v = _project(x2, seq_len, c_norm_g[0], c_w_qkv[0], c_q_norm_g[0], c_k_norm_g[0],
                       tables, plain_w=0, qk_w=D_MODEL, v_w=D_MODEL, tm=1024)
    o = _dilated_attention(to4(q), to4(k), to4(v), score_bound(c_q_norm_g[0], c_k_norm_g[0]))
    x2 = _mix_ffn(x2, rows(o), c_w_out[0], ffn_norm_g[1], ffn_w_gate[1], ffn_w_up[1],
                  ffn_w_down[1], tm=512)
    return x2.reshape(bsz, seq_len, d)
```

```python
import functools
import math

import jax
import jax.numpy as jnp
from jax import lax
from jax.experimental import pallas as pl
from jax.experimental.pallas import tpu as pltpu

D_MODEL = 1024
NORM_EPS = 1e-6
ROPE_THETA = 500000.0
HEAD_DIM = 64
ROT_DIM = HEAD_DIM // 4
ROT_HALF = ROT_DIM // 2
NEG_INF = -1e30
LOG2_E = math.log2(math.e)
RNN_WIDTH = D_MODEL // 2
RNN_BLOCKS = 8
RNN_BLOCK = RNN_WIDTH // RNN_BLOCKS
CONV_WIDTH = 4
RGLRU_C = 8.0
DIFF_HEADS = (D_MODEL // 2) // (2 * HEAD_DIM)
DIFF_QK_WIDTH = DIFF_HEADS * 2 * HEAD_DIM
DIFF_V_DIM = 2 * HEAD_DIM
DIFF_WIDTH = DIFF_HEADS * DIFF_V_DIM
DIL_HEADS = D_MODEL // HEAD_DIM
DIL_PATTERNS = ((128, 1), (512, 4), (2048, 16))
DIL_BLOCK = 128
D_FF = -(-8 * D_MODEL // (3 * 256)) * 256

LANES = 128
SUBLANES = 8
MXU_DIM = 256
VMEM_LIMIT_BYTES = 56 * 1024 * 1024

F32 = jnp.float32
BF16 = jnp.bfloat16


def _dot(a, b):
    return jnp.dot(a, b, preferred_element_type=F32)


def _dot_nt(a, b):
    return lax.dot_general(a, b, (((1,), (1,)), ((), ())), preferred_element_type=F32)


def _rms(xf, g):
    ms = jnp.mean(xf * xf, axis=-1, keepdims=True)
    return xf * lax.rsqrt(ms + NORM_EPS) * g


def _const_spec(shape):
    nd = len(shape)
    return pl.BlockSpec(shape, lambda *_: (0,) * nd, pipeline_mode=pl.Buffered(1))


def _head_group_norm_rope(t, head_mean, tab_c, tab_a, tab_b):
    inv = lax.rsqrt(_dot((t * t).astype(BF16), head_mean) + NORM_EPS)
    out = []
    for c in range(t.shape[1] // LANES):
        tc = t[:, c * LANES:(c + 1) * LANES]
        out.append(tc * tab_c + pltpu.roll(tc, ROT_HALF, 1) * tab_a
                   + pltpu.roll(tc, LANES - ROT_HALF, 1) * tab_b)
    return inv * jnp.concatenate(out, axis=1)


def _gained_tables(tables, gain, scale):
    cos_t, sin_a, sin_b = tables
    g2 = jnp.concatenate([gain, gain]).reshape(1, LANES).astype(F32)
    return (cos_t * g2 * scale, sin_a * jnp.roll(g2, ROT_HALF, axis=1) * scale,
            sin_b * jnp.roll(g2, -ROT_HALF, axis=1) * scale)


def _rope_tables(seq_len):
    lane = jnp.arange(LANES) % HEAD_DIM
    pos = jnp.arange(seq_len, dtype=F32)
    inv_freq = 1.0 / (ROPE_THETA ** ((2 * (lane % ROT_HALF)).astype(F32) / ROT_DIM))
    ang = pos[:, None] * inv_freq[None, :]
    cos, sin = jnp.cos(ang), jnp.sin(ang)
    cos_t = jnp.where(lane < ROT_DIM, cos, 1.0)
    sin_a = jnp.where((lane >= ROT_HALF) & (lane < ROT_DIM), sin, 0.0)
    sin_b = jnp.where(lane < ROT_HALF, -sin, 0.0)
    return cos_t, sin_a, sin_b


def _proj_kernel(x_ref, g_ref, w_ref, ones_ref, qc_ref, qa_ref, qb_ref, kc_ref, ka_ref, kb_ref,
                 *rest, plain_w, qk_w, v_w, v_transposed):
    h = _rms(x_ref[...], g_ref[...]).astype(BF16)
    if v_transposed:
        wvt_ref, *out_refs = rest
    else:
        out_refs = rest
    if plain_w:
        plain_ref, q_ref, k_ref, v_ref = out_refs
        plain_ref[...] = _dot(h, w_ref[:, 0:plain_w])
    else:
        q_ref, k_ref, v_ref = out_refs
    head_mean = ones_ref[...]
    per_group = MXU_DIM // LANES
    for ref, tabs, off in ((q_ref, (qc_ref, qa_ref, qb_ref), plain_w),
                           (k_ref, (kc_ref, ka_ref, kb_ref), plain_w + qk_w)):
        t_all = _dot(h, w_ref[:, off:off + qk_w])
        tab_c, tab_a, tab_b = (t[...] for t in tabs)
        for grp in range(qk_w // MXU_DIM):
            y = _head_group_norm_rope(t_all[:, grp * MXU_DIM:(grp + 1) * MXU_DIM], head_mean,
                                      tab_c, tab_a, tab_b).astype(BF16)
            for c in range(per_group):
                ref[grp * per_group + c] = y[:, c * LANES:(c + 1) * LANES]
    if v_transposed:
        v_ref[0] = _dot_nt(wvt_ref[...], h).astype(BF16)
    else:
        off = plain_w + 2 * qk_w
        v_all = _dot(h, w_ref[:, off:off + v_w]).astype(BF16)
        for c in range(v_w // LANES):
            v_ref[c] = v_all[:, c * LANES:(c + 1) * LANES]


def _project(x2, seq_len, norm_g, w, q_gain, k_gain, tables, *, plain_w, qk_w, v_w, tm,
             v_transposed=False):
    n, d = x2.shape
    n_out = plain_w + 2 * qk_w + v_w
    assert w.shape == (d, n_out) and n % tm == 0 and seq_len % tm == 0
    tiles_per_seq = seq_len // tm
    row = lambda i: (i, 0)
    pos = lambda i: (i % tiles_per_seq, 0)
    ones2 = jnp.kron(jnp.eye(MXU_DIM // HEAD_DIM, dtype=F32),
                     jnp.full((HEAD_DIM, HEAD_DIM), 1.0 / HEAD_DIM, F32)).astype(BF16)
    rope_in = (*_gained_tables(tables, q_gain, HEAD_DIM ** -0.5 * LOG2_E),
               *_gained_tables(tables, k_gain, 1.0))
    out_shape, out_specs = [], []
    if plain_w:
        out_shape.append(jax.ShapeDtypeStruct((n, plain_w), F32))
        out_specs.append(pl.BlockSpec((tm, plain_w), row))
    chunked = lambda width: (jax.ShapeDtypeStruct((width // LANES, n, LANES), BF16),
                             pl.BlockSpec((width // LANES, tm, LANES), lambda i: (0, i, 0)))
    for width in (qk_w, qk_w):
        shape, spec = chunked(width)
        out_shape.append(shape)
        out_specs.append(spec)
    operands = [x2, norm_g.reshape(1, d), w.astype(BF16), ones2, *rope_in]
    in_specs = [pl.BlockSpec((tm, d), row), _const_spec((1, d)), _const_spec((d, n_out)),
                _const_spec(ones2.shape)] + [pl.BlockSpec((tm, LANES), pos) for _ in rope_in]
    if v_transposed:
        operands.append(w[:, n_out - v_w:].T.astype(BF16))
        in_specs.append(_const_spec((v_w, d)))
        out_shape.append(jax.ShapeDtypeStruct((n // seq_len, v_w, seq_len), BF16))
        out_specs.append(pl.BlockSpec((1, v_w, tm),
                                      lambda i: (i // tiles_per_seq, 0, i % tiles_per_seq)))
    else:
        shape, spec = chunked(v_w)
        out_shape.append(shape)
        out_specs.append(spec)
    return pl.pallas_call(
        functools.partial(_proj_kernel, plain_w=plain_w, qk_w=qk_w, v_w=v_w,
                          v_transposed=v_transposed),
        grid=(n // tm,),
        in_specs=in_specs,
        out_specs=out_specs,
        out_shape=out_shape,
        compiler_params=pltpu.CompilerParams(dimension_semantics=("arbitrary",),
                                             vmem_limit_bytes=VMEM_LIMIT_BYTES),
        name="proj_qk_rope",
    )(*operands)


def _rglru_gates(x, cw_ref, cb_ref, wax_ref, tail_scr):
    ts = x.shape[0]
    xe = jnp.concatenate([tail_scr[...], x], axis=0)
    cw = cw_ref[...]
    u = cb_ref[...] + x * cw[CONV_WIDTH - 1:CONV_WIDTH]
    for back in range(1, CONV_WIDTH):
        shifted = pltpu.roll(xe, back, 0)[SUBLANES:]
        u = u + shifted * cw[CONV_WIDTH - 1 - back:CONV_WIDTH - back]
    tail_scr[...] = x[ts - SUBLANES:]

    ub = u.astype(BF16)
    half = RNN_WIDTH // 2
    ra0 = _dot(ub[:, :half], wax_ref[0])
    ra1 = _dot(ub[:, half:], wax_ref[1])
    pre_r = jnp.concatenate([ra0[:, :half], ra1[:, :half]], axis=-1)
    pre_i = jnp.concatenate([ra0[:, half:], ra1[:, half:]], axis=-1)
    return u, pre_r, pre_i


def _rglru_rows(u, pre_r, pre_i, gate, ba, bx, softplus, carry):
    rows = u.shape[0]
    r = jax.nn.sigmoid(pre_r + ba)
    i = jax.nn.sigmoid(pre_i + bx)
    log_a = (-RGLRU_C) * r * softplus
    a = jnp.exp(log_a)
    b = jnp.sqrt(-jnp.tanh(log_a) * (a * a + 1.0)) * (i * u)

    grouped = (rows // SUBLANES, SUBLANES, a.shape[1])
    a, b = a.reshape(grouped), b.reshape(grouped)
    row_in_group = lax.broadcasted_iota(jnp.int32, grouped, 1)
    step = 1
    while step < SUBLANES:
        keep = row_in_group >= step
        a_prev = jnp.where(keep, pltpu.roll(a, step, 1), 1.0)
        b_prev = jnp.where(keep, pltpu.roll(b, step, 1), 0.0)
        b = a * b_prev + b
        a = a * a_prev
        step *= 2
    a, b = a.reshape(rows, grouped[2]), b.reshape(rows, grouped[2])
    groups = []
    for grp in range(rows // SUBLANES):
        sl = slice(grp * SUBLANES, (grp + 1) * SUBLANES)
        h_grp = b[sl] + a[sl] * carry
        groups.append(h_grp)
        carry = h_grp[SUBLANES - 1:SUBLANES, :]
    h = jnp.concatenate(groups, axis=0)
    return (h * jax.nn.gelu(gate)).astype(BF16), carry


def _rglru_operands(conv_w, conv_b, wa, wx, ba, bx, lru_lambda):
    w = RNN_WIDTH
    half = w // 2
    per_half = RNN_BLOCKS // 2

    def block_diag(wt):
        blocks = wt.astype(F32).reshape(2, per_half, RNN_BLOCK, 1, RNN_BLOCK)
        keep = jnp.eye(per_half, dtype=bool).reshape(1, per_half, 1, per_half, 1)
        return jnp.where(keep, blocks, 0.0).reshape(2, half, half)

    wax = jnp.concatenate([block_diag(wa), block_diag(wx)], axis=-1).astype(BF16)
    vec = lambda v: v.reshape(1, w).astype(F32)
    operands = [conv_w.astype(F32), vec(conv_b), wax, vec(ba), vec(bx), vec(lru_lambda)]
    return operands, [_const_spec(op.shape) for op in operands]


def _diff_attn_kernel(q_ref, k_ref, v_ref, lq1_ref, lk1_ref, lq2_ref, lk2_ref, sg_ref, o_ref,
                      *, tq, tk, lambda_init):
    qi = pl.program_id(2)
    q = q_ref[...]
    lo = lax.broadcasted_iota(jnp.int32, q.shape, 1) < HEAD_DIM
    zero = jnp.zeros_like(q)
    q_parts = (jnp.where(lo, q, zero), jnp.where(lo, zero, q))
    q_pos = qi * tq + lax.broadcasted_iota(jnp.int32, (tq, tk), 0)
    k_off = lax.broadcasted_iota(jnp.int32, (tq, tk), 1)

    def body(j, carry):
        start = pl.multiple_of(j * tk, tk)
        kb = k_ref[pl.ds(start, tk), :]
        vb = v_ref[pl.ds(start, tk), :]
        visible = (k_off + start) <= q_pos
        new = []
        for c in range(2):
            m, l, acc = carry[c]
            s = jnp.where(visible, _dot_nt(q_parts[c], kb), NEG_INF)
            m_new = jnp.maximum(m, jnp.max(s, axis=-1, keepdims=True))
            alpha = jnp.exp2(m - m_new)
            e = jnp.exp2(s - m_new)
            l = alpha * l + jnp.sum(e, axis=-1, keepdims=True)
            acc = alpha * acc + _dot(e.astype(BF16), vb)
            new.append((m_new, l, acc))
        return tuple(new)

    init = tuple((jnp.full((tq, 1), NEG_INF, F32), jnp.zeros((tq, 1), F32),
                  jnp.zeros((tq, DIFF_V_DIM), F32)) for _ in range(2))
    n_kv = (qi * tq + tq + tk - 1) // tk
    (_, l0, acc0), (_, l1, acc1) = lax.fori_loop(0, n_kv, body, init)

    dot_l = lambda a, b: jnp.sum(a[...] * b[...], axis=-1, keepdims=True)
    lam = jnp.exp(dot_l(lq1_ref, lk1_ref)) - jnp.exp(dot_l(lq2_ref, lk2_ref)) + lambda_init
    o = acc0 / l0 - lam * (acc1 / l1)
    o_ref[...] = (_rms(o, sg_ref[...]) * (1.0 - lambda_init)).astype(BF16)


def _diff_fast_kernel(q_ref, k_ref, vt_ref, lq1_ref, lk1_ref, lq2_ref, lk2_ref, sg_ref, o_ref,
                      s_scr, *, seq_len, tq, lambda_init):
    dot_l = lambda a, b: jnp.sum(a[...] * b[...], axis=-1, keepdims=True)
    lam = jnp.exp(dot_l(lq1_ref, lk1_ref)) - jnp.exp(dot_l(lq2_ref, lk2_ref)) + lambda_init
    lo = lax.broadcasted_iota(jnp.int32, (tq, LANES), 1) < HEAD_DIM
    key_row = lax.broadcasted_iota(jnp.int32, (tq, 2 * tq), 0)
    q_col = lax.broadcasted_iota(jnp.int32, (tq, 2 * tq), 1) % tq
    causal = key_row <= q_col
    for i in range(seq_len // tq):
        qt = q_ref[i * tq:(i + 1) * tq, :]
        zero = jnp.zeros_like(qt)
        q2 = jnp.concatenate([jnp.where(lo, qt, zero), jnp.where(lo, zero, qt)], axis=0)
        n_keys = (i + 1) * tq
        s_scr[0:n_keys, :] = _dot_nt(k_ref[0:n_keys, :], q2)
        acc = jnp.zeros((DIFF_V_DIM, 2 * tq), F32)
        l8 = jnp.zeros((SUBLANES, 2 * tq), F32)
        for j in range(i + 1):
            s = s_scr[j * tq:(j + 1) * tq, :]
            if j == i:
                s = jnp.where(causal, s, NEG_INF)
            e = jnp.exp2(s)
            l8 = l8 + jnp.sum(e.reshape(tq // SUBLANES, SUBLANES, 2 * tq), axis=0)
            acc = acc + _dot(vt_ref[0, :, j * tq:(j + 1) * tq], e.astype(BF16))
        l = jnp.sum(l8, axis=0, keepdims=True)
        o = acc[:, :tq] / l[:, :tq] - lam * (acc[:, tq:] / l[:, tq:])
        ms = jnp.mean(o * o, axis=0, keepdims=True)
        y = o * lax.rsqrt(ms + NORM_EPS) * sg_ref[...] * (1.0 - lambda_init)
        o_ref[i * tq:(i + 1) * tq, :] = y.T.astype(BF16)


FAST_SCORE_BOUND = 60.0


def _diff_attention(q4, k4, vt3, lq1, lk1, lq2, lk2, subln_g, score_bound, *, layer_idx, tq, tk):
    _, bsz, seq_len, _ = q4.shape
    lambda_init = 0.8 - 0.6 * math.exp(-0.3 * layer_idx)
    vec = lambda v: v.reshape(1, -1).astype(F32)
    lam_ops = (vec(lq1), vec(lk1), vec(lq2), vec(lk2))
    lam_specs = [_const_spec((1, HEAD_DIM)) for _ in range(4)]
    seq_spec = pl.BlockSpec((None, None, seq_len, LANES), lambda b, h: (h, b, 0, 0))
    out_shape = jax.ShapeDtypeStruct((DIFF_HEADS, bsz, seq_len, LANES), BF16)

    def fast(q3, k3, vt3):
        gain_cols = jnp.broadcast_to(subln_g.astype(F32)[:, None], (DIFF_V_DIM, tq))
        return pl.pallas_call(
            functools.partial(_diff_fast_kernel, seq_len=seq_len, tq=tq, lambda_init=lambda_init),
            grid=(bsz, DIFF_HEADS),
            in_specs=[seq_spec, seq_spec, pl.BlockSpec((1, LANES, seq_len), lambda b, h: (b, h, 0)),
                      *lam_specs, _const_spec((DIFF_V_DIM, tq))],
            out_specs=seq_spec,
            out_shape=out_shape,
            scratch_shapes=[pltpu.VMEM((seq_len, 2 * tq), F32)],
            compiler_params=pltpu.CompilerParams(dimension_semantics=("arbitrary", "arbitrary"),
                                                 vmem_limit_bytes=VMEM_LIMIT_BYTES),
            name="diff_attn_fast",
        )(q3, k3, vt3, *lam_ops, gain_cols)

    def safe(q3, k3, vt3):
        kv_spec = pl.BlockSpec((None, None, seq_len, LANES), lambda b, h, i: (h, b, 0, 0))
        q_spec = pl.BlockSpec((None, None, tq, LANES), lambda b, h, i: (h, b, i, 0))
        v4 = vt3.reshape(bsz, DIFF_HEADS, DIFF_V_DIM, seq_len).transpose(1, 0, 3, 2)
        return pl.pallas_call(
            functools.partial(_diff_attn_kernel, tq=tq, tk=tk, lambda_init=lambda_init),
            grid=(bsz, DIFF_HEADS, seq_len // tq),
            in_specs=[q_spec, kv_spec, kv_spec, *lam_specs, _const_spec((1, DIFF_V_DIM))],
            out_specs=q_spec,
            out_shape=out_shape,
            compiler_params=pltpu.CompilerParams(
                dimension_semantics=("arbitrary", "arbitrary", "arbitrary"),
                vmem_limit_bytes=VMEM_LIMIT_BYTES),
            name="diff_attn",
        )(q3, k3, v4, *lam_ops, vec(subln_g))

    return lax.cond(score_bound <= FAST_SCORE_BOUND, fast, safe, q4, k4, vt3)


def _dil_unit(q0, q1, k2, va, vb, band):
    lo = lax.broadcasted_iota(jnp.int32, (DIL_BLOCK, LANES), 1) < HEAD_DIM
    res, maxes = [], []
    for qh, vh in ((q0, va), (q1, vb)):
        s = jnp.where(band, _dot_nt(qh, k2), NEG_INF)
        m = s[:, :LANES]
        for c in range(1, s.shape[1] // LANES):
            m = jnp.maximum(m, s[:, c * LANES:(c + 1) * LANES])
        m = jnp.max(m, axis=-1, keepdims=True)
        e = jnp.exp2(s - m).astype(BF16)
        res.append(_dot(e, vh))
        maxes.append(m)
    acc = jnp.where(lo, res[0], res[1])
    l = pltpu.roll(jnp.where(lo, res[1], res[0]), HEAD_DIM, 1)
    return acc, jnp.where(lo, maxes[0], maxes[1]), l


def _dil_attn_kernel(q_ref, k_ref, v_ref, o_ref, qf, kf, vf, q0c, q1c, kc, vac, vbc,
                     *nat, seq_len):
    blk = DIL_BLOCK
    n_pat = len(DIL_PATTERNS)
    acc_n, m_n, l_n = nat[:n_pat], nat[n_pat:2 * n_pat], nat[2 * n_pat:]
    qf[...] = q_ref[...].astype(F32)
    kf[...] = k_ref[...].astype(F32)
    vf[...] = v_ref[...].astype(F32)

    qi = lax.broadcasted_iota(jnp.int32, (blk, blk), 0)
    ki = lax.broadcasted_iota(jnp.int32, (blk, blk), 1)
    band_cur = ki <= qi
    band_two = jnp.concatenate([ki >= qi, band_cur], axis=1)

    for g, (window, dil) in enumerate(DIL_PATTERNS):
        assert window // dil == blk and seq_len % (blk * dil) == 0
        lc = seq_len // dil
        for r in range(dil):
            rows = pl.ds(r, lc, stride=dil) if dil > 1 else pl.ds(0, lc)
            qv, vv = qf[rows, :], vf[rows, :]
            lo = lax.broadcasted_iota(jnp.int32, qv.shape, 1) < HEAD_DIM
            q0c[0:lc, :] = jnp.where(lo, qv, 0.0).astype(BF16)
            q1c[0:lc, :] = jnp.where(lo, 0.0, qv).astype(BF16)
            kc[0:lc, :] = kf[rows, :].astype(BF16)
            vac[0:lc, :] = jnp.where(lo, vv, 1.0).astype(BF16)
            vbc[0:lc, :] = jnp.where(lo, 1.0, vv).astype(BF16)
            for n in range(lc // blk):
                cur = slice(n * blk, (n + 1) * blk)
                keys = slice(max(n - 1, 0) * blk, (n + 1) * blk)
                acc, m, l = _dil_unit(q0c[cur, :], q1c[cur, :], kc[keys, :], vac[keys, :],
                                      vbc[keys, :], band_two if n else band_cur)
                first = n * blk * dil + r
                dst = pl.ds(first, blk, stride=dil) if dil > 1 else pl.ds(first, blk)
                acc_n[g][dst, :] = acc
                m_n[g][dst, :] = m
                l_n[g][dst, :] = l

    chunk = 256
    for c in range(seq_len // chunk):
        sl = slice(c * chunk, (c + 1) * chunk)
        ms = [m_n[g][sl, :] for g in range(n_pat)]
        m_all = functools.reduce(jnp.maximum, ms)
        num = jnp.zeros((chunk, LANES), F32)
        den = jnp.zeros((chunk, LANES), F32)
        for g in range(n_pat):
            wgt = jnp.exp2(ms[g] - m_all)
            num = num + wgt * acc_n[g][sl, :]
            den = den + wgt * l_n[g][sl, :]
        o_ref[sl, :] = (num / den).astype(BF16)


DIL_PAIRS_PER_STEP = 1


def _dil_fast_kernel(q_ref, k_ref, v_ref, o_ref, bias_scr, *bufs, seq_len):
    blk = DIL_BLOCK
    qi = lax.broadcasted_iota(jnp.int32, (2 * blk, blk), 0) % blk
    ki = lax.broadcasted_iota(jnp.int32, (2 * blk, blk), 1)
    bias_scr[:, 0:blk] = jnp.where(ki >= qi, 0.0, NEG_INF)
    bias_scr[:, blk:] = jnp.where(ki <= qi, 0.0, NEG_INF)
    for pair in range(q_ref.shape[0]):
        _dil_fast_pair(q_ref.at[pair], k_ref.at[pair], v_ref.at[pair], o_ref.at[pair], bias_scr,
                       bufs, seq_len)


def _dil_fast_pair(q_ref, k_ref, v_ref, o_ref, bias_scr, bufs, seq_len):
    blk = DIL_BLOCK
    n_pat = len(DIL_PATTERNS)
    natural, staged = bufs[0:3], bufs[3:6]
    out_scr = bufs[6]
    cls = [bufs[7 + 5 * g:12 + 5 * g] for g in range(n_pat)]
    res = bufs[7 + 5 * n_pat:]
    acc_n, l_n = res[:n_pat], res[n_pat:]
    dilated = [dil for _, dil in DIL_PATTERNS if dil > 1]
    merge_dil = min(dilated) if dilated else 1
    assert all(dil % merge_dil == 0 for dil in dilated)

    lo_blk = lax.broadcasted_iota(jnp.int32, (blk, LANES), 1) < HEAD_DIM

    if any(dil > 1 for _, dil in DIL_PATTERNS):
        for dst, src in zip(natural, (q_ref, k_ref, v_ref)):
            dst[...] = src[...].astype(F32)

    staged_dil = 1
    for g, (window, dil) in enumerate(DIL_PATTERNS):
        assert window // dil == blk and seq_len % (blk * dil) == 0
        lc = seq_len // dil
        q0c, q1c, kc, vac, vbc = cls[g]
        refine = staged_dil > 1 and dil % staged_dil == 0
        stage_here = (dil > 1 and not refine
                      and any(d2 > dil and d2 % dil == 0 for _, d2 in DIL_PATTERNS[g + 1:]))
        for r in range(dil):
            base = r * lc
            if dil == 1:
                qv, kv, vv = q_ref[...], k_ref[...], v_ref[...]
            else:
                if refine:
                    first = (r % staged_dil) * (seq_len // staged_dil) + r // staged_dil
                    rows, srcs = pl.ds(first, lc, stride=dil // staged_dil), staged
                else:
                    rows, srcs = pl.ds(r, lc, stride=dil), natural
                qv, kv, vv = (s[rows, :] for s in srcs)
                if stage_here:
                    for dst, val in zip(staged, (qv, kv, vv)):
                        dst[base:base + lc, :] = val
            lo = lax.broadcasted_iota(jnp.int32, qv.shape, 1) < HEAD_DIM
            zero, one = jnp.zeros_like(qv), jnp.ones_like(vv)
            q0c[base:base + lc, :] = jnp.where(lo, qv, zero).astype(BF16)
            q1c[base:base + lc, :] = jnp.where(lo, zero, qv).astype(BF16)
            kc[base:base + lc, :] = kv.astype(BF16)
            vac[base:base + lc, :] = jnp.where(lo, vv, one).astype(BF16)
            vbc[base:base + lc, :] = jnp.where(lo, one, vv).astype(BF16)
            for n in range(lc // blk):
                cur = slice(base + n * blk, base + (n + 1) * blk)
                keys = slice(base + max(n - 1, 0) * blk, base + (n + 1) * blk)
                q2 = jnp.concatenate([q0c[cur, :], q1c[cur, :]], axis=0)
                bias = bias_scr[...] if n else bias_scr[:, blk:]
                e = jnp.exp2(_dot_nt(q2, kc[keys, :]) + bias).astype(BF16)
                ra = _dot(e[:blk], vac[keys, :])
                rb = _dot(e[blk:], vbc[keys, :])
                if dil == 1:
                    dst = pl.ds(n * blk, blk)
                else:
                    sub = dil // merge_dil
                    first = ((r % merge_dil) * (seq_len // merge_dil) + r // merge_dil
                             + n * blk * sub)
                    dst = pl.ds(first, blk, stride=sub) if sub > 1 else pl.ds(first, blk)
                acc_n[g][dst, :] = jnp.where(lo_blk, ra, rb)
                l_n[g][dst, :] = jnp.where(lo_blk, rb, ra)
        if stage_here:
            staged_dil = dil

    lcm = seq_len // merge_dil
    chunk = 256
    for r in range(merge_dil):
        for c in range(lcm // chunk):
            cm_rows = pl.ds(r * lcm + c * chunk, chunk)
            nat_rows = (pl.ds(r + c * chunk * merge_dil, chunk, stride=merge_dil)
                        if merge_dil > 1 else cm_rows)
            rows_of = lambda g: nat_rows if DIL_PATTERNS[g][1] == 1 else cm_rows
            num = functools.reduce(jnp.add, [acc_n[g][rows_of(g), :] for g in range(n_pat)])
            den = functools.reduce(jnp.add, [l_n[g][rows_of(g), :] for g in range(n_pat)])
            out_scr[nat_rows, :] = num / pltpu.roll(den, HEAD_DIM, 1)
    for c in range(seq_len // chunk):
        sl = slice(c * chunk, (c + 1) * chunk)
        o_ref[sl, :] = out_scr[sl, :].astype(BF16)


def _dilated_attention(q4, k4, v4, score_bound):
    n_pairs, bsz, seq_len, _ = q4.shape
    n_pat = len(DIL_PATTERNS)

    def call(body, pairs_per_step, scratch_shapes, name):
        lead = pairs_per_step
        pairs_per_step = pairs_per_step or 1
        assert n_pairs % pairs_per_step == 0
        spec = pl.BlockSpec((lead, None, seq_len, LANES), lambda b, h: (h, b, 0, 0))
        return pl.pallas_call(
            functools.partial(body, seq_len=seq_len),
            grid=(bsz, n_pairs // pairs_per_step),
            in_specs=[spec, spec, spec],
            out_specs=spec,
            out_shape=jax.ShapeDtypeStruct(q4.shape, BF16),
            scratch_shapes=scratch_shapes,
            compiler_params=pltpu.CompilerParams(dimension_semantics=("arbitrary", "arbitrary"),
                                                 vmem_limit_bytes=VMEM_LIMIT_BYTES),
            name=name)

    seq_f32 = lambda count: [pltpu.VMEM((seq_len, LANES), F32) for _ in range(count)]
    seq_bf16 = lambda count: [pltpu.VMEM((seq_len, LANES), BF16) for _ in range(count)]
    fast = call(_dil_fast_kernel, DIL_PAIRS_PER_STEP,
                [pltpu.VMEM((2 * DIL_BLOCK, 2 * DIL_BLOCK), F32)]
                + seq_f32(7) + seq_bf16(5 * n_pat) + seq_f32(2 * n_pat), "dilated_attn_fast")
    safe = call(_dil_attn_kernel, None, seq_f32(3) + seq_bf16(5) + seq_f32(3 * n_pat),
                "dilated_attn")
    return lax.cond(score_bound <= FAST_SCORE_BOUND, fast, safe, q4, k4, v4)


FFN_SUB_ROWS = 256


def _mix_ffn_tile(x, ys, wo_ref, g_ref, wg_ref, wu_ref, wd_ref, side_work=(), n_split=1):
    rows = x.shape[0] // n_split
    subs = [slice(s * rows, (s + 1) * rows) for s in range(n_split)]
    x1s, hs = [], []
    for sub in subs:
        x1 = x[sub]
        off = 0
        for y in ys:
            width = y.shape[1]
            x1 = x1 + _dot(y[sub], wo_ref[off:off + width, :])
            off += width
        x1s.append(x1)
        hs.append(_rms(x1, g_ref[...]).astype(BF16))
    n_chunks = wg_ref.shape[1] // MXU_DIM
    slots = n_chunks * n_split
    acts = [[] for _ in subs]
    slot = 0
    for c in range(n_chunks):
        cols = slice(c * MXU_DIM, (c + 1) * MXU_DIM)
        for s in range(n_split):
            gate = _dot(hs[s], wg_ref[:, cols])
            up = _dot(hs[s], wu_ref[:, cols])
            acts[s].append((gate * jax.nn.sigmoid(gate) * up).astype(BF16))
            for work in side_work[slot * len(side_work) // slots:
                                  (slot + 1) * len(side_work) // slots]:
                work(gate[0:1, :])
            slot += 1
    outs = [x1s[s] + _dot(jnp.concatenate(acts[s], axis=-1), wd_ref[...]) for s in range(n_split)]
    return outs[0] if n_split == 1 else jnp.concatenate(outs, axis=0)


def _chunks_to_lanes(y_ref):
    return jnp.concatenate([y_ref[c] for c in range(y_ref.shape[0])], axis=-1)


def _mix_ffn_kernel(x_ref, y_ref, wo_ref, g_ref, wg_ref, wu_ref, wd_ref, o_ref):
    o_ref[...] = _mix_ffn_tile(x_ref[...], [_chunks_to_lanes(y_ref)], wo_ref, g_ref, wg_ref,
                               wu_ref, wd_ref, n_split=x_ref.shape[0] // FFN_SUB_ROWS)


RGLRU_PIECE_ROWS = 16


def _rglru_mix_ffn_kernel(x_ref, xr_ref, gate_ref, yd_ref, cw_ref, cb_ref, wax_ref, ba_ref, bx_ref,
                          lam_ref, wo_ref, g_ref, wg_ref, wu_ref, wd_ref, o_ref,
                          tail_scr, h_scr, y_scr, *, tiles_per_seq):
    j = pl.program_id(0)

    @pl.when(j == 0)
    def _():
        y_scr[...] = jnp.zeros_like(y_scr)

    @pl.when(j % tiles_per_seq == 0)
    def _():
        tail_scr[...] = jnp.zeros_like(tail_scr)
        h_scr[...] = jnp.zeros_like(h_scr)

    y_prev = y_scr[...]
    u, pre_r, pre_i = _rglru_gates(xr_ref[...], cw_ref, cb_ref, wax_ref, tail_scr)
    z = -lam_ref[...]
    softplus = jnp.maximum(z, 0.0) + jnp.log1p(jnp.exp(-jnp.abs(z)))
    ba, bx = ba_ref[...], bx_ref[...]
    state = [h_scr[0:1, :]]

    def piece(p):
        def run(anchor):
            floor = jnp.minimum(jnp.concatenate([anchor] * (RNN_WIDTH // anchor.shape[1]), axis=1),
                                -3.0e38)
            sl = slice(p * RGLRU_PIECE_ROWS, (p + 1) * RGLRU_PIECE_ROWS)
            y_rows, state[0] = _rglru_rows(u[sl], jnp.maximum(pre_r[sl], floor),
                                           jnp.maximum(pre_i[sl], floor), gate_ref[sl, :], ba, bx,
                                           softplus, state[0])
            y_scr[sl, :] = y_rows
        return run

    n_pieces = xr_ref.shape[0] // RGLRU_PIECE_ROWS
    o_ref[...] = _mix_ffn_tile(x_ref[...], [y_prev, _chunks_to_lanes(yd_ref)], wo_ref, g_ref,
                               wg_ref, wu_ref, wd_ref,
                               side_work=[piece(p) for p in range(n_pieces)],
                               n_split=x_ref.shape[0] // FFN_SUB_ROWS)
    h_scr[...] = jnp.broadcast_to(state[0], h_scr.shape)


def _ffn_operands(w_out, norm_g, w_gate, w_up, w_down):
    d = norm_g.shape[0]
    operands = [w_out.astype(BF16), norm_g.reshape(1, d), w_gate.astype(BF16), w_up.astype(BF16),
                w_down.astype(BF16)]
    return operands, [_const_spec(op.shape) for op in operands]


def _mix_ffn(x2, y, w_out, norm_g, w_gate, w_up, w_down, *, tm):
    n, d = x2.shape
    row = lambda i: (i, 0)
    ffn_ops, ffn_specs = _ffn_operands(w_out, norm_g, w_gate, w_up, w_down)
    return pl.pallas_call(
        _mix_ffn_kernel,
        grid=(n // tm,),
        in_specs=[pl.BlockSpec((tm, d), row),
                  pl.BlockSpec((y.shape[0], tm, LANES), lambda i: (0, i, 0)), *ffn_specs],
        out_specs=pl.BlockSpec((tm, d), row),
        out_shape=jax.ShapeDtypeStruct((n, d), F32),
        compiler_params=pltpu.CompilerParams(dimension_semantics=("arbitrary",),
                                             vmem_limit_bytes=VMEM_LIMIT_BYTES),
        name="outproj_swiglu",
    )(x2, y, *ffn_ops)


def _rglru_mix_ffn(x2, xg, y_diff, seq_len, rglru_params, w_out, norm_g, w_gate, w_up, w_down,
                   *, tm):
    n, d = x2.shape
    w = RNN_WIDTH
    n_tiles = n // tm
    assert seq_len % tm == 0
    prev = lambda j: jnp.maximum(j - 1, 0)
    this = lambda j: jnp.minimum(j, n_tiles - 1)
    rg_ops, rg_specs = _rglru_operands(*rglru_params)
    ffn_ops, ffn_specs = _ffn_operands(w_out, norm_g, w_gate, w_up, w_down)
    return pl.pallas_call(
        functools.partial(_rglru_mix_ffn_kernel, tiles_per_seq=seq_len // tm),
        grid=(n_tiles + 1,),
        in_specs=[pl.BlockSpec((tm, d), lambda j: (prev(j), 0)),
                  pl.BlockSpec((tm, w), lambda j: (this(j), 0)),
                  pl.BlockSpec((tm, w), lambda j: (this(j), 1)),
                  pl.BlockSpec((y_diff.shape[0], tm, LANES), lambda j: (0, prev(j), 0)),
                  *rg_specs, *ffn_specs],
        out_specs=pl.BlockSpec((tm, d), lambda j: (prev(j), 0)),
        out_shape=jax.ShapeDtypeStruct((n, d), F32),
        scratch_shapes=[pltpu.VMEM((SUBLANES, w), F32), pltpu.VMEM((SUBLANES, w), F32),
                        pltpu.VMEM((tm, w), BF16)],
        compiler_params=pltpu.CompilerParams(dimension_semantics=("arbitrary",),
                                             vmem_limit_bytes=VMEM_LIMIT_BYTES),
        name="rglru_outproj_swiglu",
    )(x2, xg, xg, y_diff, *rg_ops, *ffn_ops)


def kernel(x, ab_norm_g, ab_w_in, ab_conv_w, ab_conv_b, ab_wa, ab_ba, ab_wx, ab_bx, ab_lru_lambda, ab_q_norm_g, ab_k_norm_g, ab_lambda_q1, ab_lambda_k1, ab_lambda_q2, ab_lambda_k2, ab_subln_g, ab_w_out, c_norm_g, c_w_qkv, c_q_norm_g, c_k_norm_g, c_w_out, ffn_norm_g, ffn_w_gate, ffn_w_up, ffn_w_down):
    bsz, seq_len, d = x.shape
    n = bsz * seq_len
    tables = _rope_tables(seq_len)
    x2 = x.reshape(n, d)

    xg, q, k, vt = _project(x2, seq_len, ab_norm_g[0], ab_w_in[0], ab_q_norm_g[0], ab_k_norm_g[0],
                            tables, plain_w=2 * RNN_WIDTH, qk_w=DIFF_QK_WIDTH, v_w=DIFF_WIDTH,
                            tm=1024, v_transposed=True)
    to4 = lambda t: t.reshape(-1, bsz, seq_len, LANES)
    rows = lambda t: t.reshape(-1, n, LANES)
    score_bound = lambda gq, gk: (HEAD_DIM ** 0.5) * jnp.max(jnp.abs(gq)) * jnp.max(jnp.abs(gk))
    y_diff = _diff_attention(to4(q), to4(k), vt, ab_lambda_q1[0], ab_lambda_k1[0],
                             ab_lambda_q2[0], ab_lambda_k2[0], ab_subln_g[0],
                             score_bound(ab_q_norm_g[0], ab_k_norm_g[0]),
                             layer_idx=0, tq=256, tk=256)
    rglru_params = (ab_conv_w[0], ab_conv_b[0], ab_wa[0], ab_wx[0], ab_ba[0], ab_bx[0],
                    ab_lru_lambda[0])
    x2 = _rglru_mix_ffn(x2, xg, rows(y_diff), seq_len, rglru_params, ab_w_out[0], ffn_norm_g[0],
                        ffn_w_gate[0], ffn_w_up[0], ffn_w_down[0], tm=512)

    q, k, v = _project(x2, seq_len, c_norm_g[0], c_w_qkv[0], c_q_norm_g[0], c_k_norm_g[0],
                       tables, plain_w=0, qk_w=D_MODEL, v_w=D_MODEL, tm=1024)
    o = _dilated_attention(to4(q), to4(k), to4(v), score_bound(c_q_norm_g[0], c_k_norm_g[0]))
    x2 = _mix_ffn(x2, rows(o), c_w_out[0], ffn_norm_g[1], ffn_w_gate[1], ffn_w_up[1],
                  ffn_w_down[1], tm=1024)
    return x2.reshape(bsz, seq_len, d)
```
